```python
import math
import jax, jax.numpy as jnp
from jax import lax
import numpy as np

D_MODEL = 1024
BATCH = 4
SEQ = 4096
DEPTH = 4

N_MIXERS = 3
EXPAND = 2
D_INNER = EXPAND * D_MODEL
NORM_EPS = 1e-6

GMLP_CHUNK = 128
GMLP_GROUPS = 8

S5_GROUP = 16
S5_STATE = 64
S5_GROUPS = D_INNER // S5_GROUP
S5_DT_MIN = 1e-3
S5_DT_MAX = 1e-1

MLA_HEADS = 16
MLA_NOPE = 128
MLA_ROPE = 64
MLA_V = D_INNER // MLA_HEADS
MLA_QK_DIM = MLA_NOPE + MLA_ROPE
MLA_Q_RANK = 384
MLA_KV_RANK = 128
MLA_SCALE = MLA_QK_DIM ** -0.5
ROPE_THETA = 10000.0
ATTN_QBLOCK = 128
NEG_INF = -1e30

kernel_name = "hybrid_gmlp_s5_mla_gated"


def _rmsnorm(x, g):
    xf = x.astype(jnp.float32)
    y = xf * lax.rsqrt(jnp.mean(xf * xf, axis=-1, keepdims=True) + NORM_EPS)
    return (y * g.astype(jnp.float32)).astype(x.dtype)


def _layernorm(x, g, b):
    xf = x.astype(jnp.float32)
    mu = jnp.mean(xf, axis=-1, keepdims=True)
    xc = xf - mu
    var = jnp.mean(xc * xc, axis=-1, keepdims=True)
    y = xc * lax.rsqrt(var + NORM_EPS) * g.astype(jnp.float32) + b.astype(jnp.float32)
    return y.astype(x.dtype)


def _rope(x, cos, sin):
    half = x.shape[-1] // 2
    x1 = x[..., :half].astype(jnp.float32)
    x2 = x[..., half:].astype(jnp.float32)
    return jnp.concatenate([x1 * cos - x2 * sin, x2 * cos + x1 * sin], axis=-1).astype(x.dtype)


def _gmlp_mixer(h, w_in, ln_g, ln_b, w_s, b_s, w_out):
    bsz, seq, _ = h.shape
    u, v, z = jnp.split(h @ w_in, 3, axis=-1)
    u = jax.nn.gelu(u)
    v = _layernorm(jax.nn.gelu(v), ln_g, ln_b)
    v = v.reshape(bsz, seq // GMLP_CHUNK, GMLP_CHUNK, GMLP_GROUPS, D_INNER // GMLP_GROUPS)
    causal = jnp.tril(jnp.ones((GMLP_CHUNK, GMLP_CHUNK), dtype=bool))
    w = jnp.where(causal[None], w_s, jnp.zeros((), w_s.dtype))
    s = jnp.einsum('gts,bcsgd->bctgd', w, v) + b_s.T[:, :, None]
    s = s.reshape(bsz, seq, D_INNER)
    return (u * s * jax.nn.silu(z)) @ w_out


def _s5_combine(left, right):
    a_l, b_l = left
    a_r, b_r = right
    return a_r * a_l, a_r * b_l + b_r


def _s5_mixer(h, w_in, a_re, a_im, log_step, b_re, b_im, c_re, c_im, d_skip, w_glu, b_glu, w_out):
    bsz, seq, _ = h.shape
    u, z = jnp.split(h @ w_in, 2, axis=-1)
    uf = u.astype(jnp.float32).reshape(bsz, seq, S5_GROUPS, S5_GROUP)
    lam = lax.complex(a_re.astype(jnp.float32), a_im.astype(jnp.float32))
    step = jnp.exp(log_step.astype(jnp.float32))[:, None]
    lam_bar = jnp.exp(lam * step)
    bmat = lax.complex(b_re.astype(jnp.float32), b_im.astype(jnp.float32))
    b_bar = ((lam_bar - 1.0) / lam)[..., None] * bmat
    bu = lax.complex(jnp.einsum('blgh,gph->lbgp', uf, jnp.real(b_bar)),
                     jnp.einsum('blgh,gph->lbgp', uf, jnp.imag(b_bar)))
    a_elems = jnp.broadcast_to(lam_bar, (seq, 1, S5_GROUPS, S5_STATE))
    _, xs = lax.associative_scan(_s5_combine, (a_elems, bu), axis=0)
    y = (jnp.einsum('lbgp,ghp->blgh', jnp.real(xs), c_re.astype(jnp.float32))
         - jnp.einsum('lbgp,ghp->blgh', jnp.imag(xs), c_im.astype(jnp.float32)))
    y = y + d_skip.astype(jnp.float32).reshape(S5_GROUPS, S5_GROUP) * uf
    y = jax.nn.gelu(y.reshape(bsz, seq, D_INNER)).astype(h.dtype)
    y = y * jax.nn.sigmoid(y @ w_glu + b_glu)
    return (y * jax.nn.silu(z)) @ w_out


def _mla_mixer(h, positions, w_in, q_norm_g, w_uq, kv_norm_g, w_ukv, w_out):
    bsz, seq, _ = h.shape
    c_q, c_kv, k_r, z = jnp.split(
        h @ w_in, [MLA_Q_RANK, MLA_Q_RANK + MLA_KV_RANK, MLA_Q_RANK + MLA_KV_RANK + MLA_ROPE], axis=-1)
    q = (_rmsnorm(c_q, q_norm_g) @ w_uq).reshape(bsz, seq, MLA_HEADS, MLA_QK_DIM)
    q_nope, q_rope = q[..., :MLA_NOPE], q[..., MLA_NOPE:]
    kv = (_rmsnorm(c_kv, kv_norm_g) @ w_ukv).reshape(bsz, seq, MLA_HEADS, MLA_NOPE + MLA_V)
    k_nope, v = kv[..., :MLA_NOPE], kv[..., MLA_NOPE:]
    inv_freq = ROPE_THETA ** (-jnp.arange(0, MLA_ROPE, 2, dtype=jnp.float32) / MLA_ROPE)
    ang = positions.astype(jnp.float32)[..., None] * inv_freq
    cos, sin = jnp.cos(ang), jnp.sin(ang)
    q_rope = _rope(q_rope, cos[:, :, None], sin[:, :, None])
    k_r = _rope(k_r, cos, sin)
    n_blk = seq // ATTN_QBLOCK

    def to_blocks(t):
        return t.reshape(bsz, n_blk, ATTN_QBLOCK, *t.shape[2:]).swapaxes(0, 1)

    kpos = jnp.arange(seq)

    def attend(args):
        qn_b, qr_b, blk = args
        s = (jnp.einsum('bqhd,bkhd->bhqk', qn_b, k_nope)
             + jnp.einsum('bqhd,bkd->bhqk', qr_b, k_r)).astype(jnp.float32) * MLA_SCALE
        qpos = blk * ATTN_QBLOCK + jnp.arange(ATTN_QBLOCK)
        s = jnp.where(kpos[None, :] <= qpos[:, None], s, NEG_INF)
        p = jax.nn.softmax(s, axis=-1).astype(v.dtype)
        return jnp.einsum('bhqk,bkhd->bqhd', p, v)

    o = lax.map(attend, (to_blocks(q_nope), to_blocks(q_rope), jnp.arange(n_blk)))
    o = o.swapaxes(0, 1).reshape(bsz, seq, MLA_HEADS * MLA_V)
    return (o * jax.nn.silu(z)) @ w_out


def _gain(key, n):
    return 1.0 + 0.02 * jax.random.normal(key, (n,), jnp.float32)


def _normal(key, shape, scale):
    return jax.random.normal(key, shape, jnp.float32) * scale


def _gmlp_params(key, p):
    k = jax.random.split(key, 7)
    return {
        p + 'norm_g': _gain(k[0], D_MODEL),
        p + 'w_in': _normal(k[1], (D_MODEL, 3 * D_INNER), D_MODEL ** -0.5),
        p + 'ln_g': _gain(k[2], D_INNER),
        p + 'ln_b': _normal(k[3], (D_INNER,), 0.02),
        p + 'w_s': _normal(k[4], (GMLP_GROUPS, GMLP_CHUNK, GMLP_CHUNK), GMLP_CHUNK ** -0.5),
        p + 'b_s': 1.0 + _normal(k[5], (GMLP_GROUPS, GMLP_CHUNK), 0.02),
        p + 'w_out': _normal(k[6], (D_INNER, D_MODEL), D_INNER ** -0.5),
    }


def _s5_params(key, p):
    k = jax.random.split(key, 14)
    n = jnp.arange(S5_STATE, dtype=jnp.float32)
    return {
        p + 'norm_g': _gain(k[0], D_MODEL),
        p + 'w_in': _normal(k[1], (D_MODEL, 2 * D_INNER), D_MODEL ** -0.5),
        p + 'a_re': -0.5 + _normal(k[2], (S5_GROUPS, S5_STATE), 0.01),
        p + 'a_im': math.pi * n[None, :] + _normal(k[3], (S5_GROUPS, S5_STATE), 0.01),
        p + 'log_step': jax.random.uniform(k[4], (S5_GROUPS,), jnp.float32,
                                           math.log(S5_DT_MIN), math.log(S5_DT_MAX)),
        p + 'b_re': _normal(k[5], (S5_GROUPS, S5_STATE, S5_GROUP), (2 * S5_GROUP) ** -0.5),
        p + 'b_im': _normal(k[6], (S5_GROUPS, S5_STATE, S5_GROUP), (2 * S5_GROUP) ** -0.5),
        p + 'c_re': _normal(k[7], (S5_GROUPS, S5_GROUP, S5_STATE), (2 * S5_STATE) ** -0.5),
        p + 'c_im': _normal(k[8], (S5_GROUPS, S5_GROUP, S5_STATE), (2 * S5_STATE) ** -0.5),
        p + 'd_skip': _normal(k[9], (D_INNER,), 1.0),
        p + 'w_glu': _normal(k[10], (D_INNER, D_INNER), D_INNER ** -0.5),
        p + 'b_glu': _normal(k[11], (D_INNER,), 0.02),
        p + 'w_out': _normal(k[12], (D_INNER, D_MODEL), D_INNER ** -0.5),
    }


def _mla_params(key, p):
    k = jax.random.split(key, 7)
    return {
        p + 'norm_g': _gain(k[0], D_MODEL),
        p + 'w_in': _normal(k[1], (D_MODEL, MLA_Q_RANK + MLA_KV_RANK + MLA_ROPE + D_INNER), D_MODEL ** -0.5),
        p + 'q_norm_g': _gain(k[2], MLA_Q_RANK),
        p + 'w_uq': _normal(k[3], (MLA_Q_RANK, MLA_HEADS * MLA_QK_DIM), MLA_Q_RANK ** -0.5),
        p + 'kv_norm_g': _gain(k[4], MLA_KV_RANK),
        p + 'w_ukv': _normal(k[5], (MLA_KV_RANK, MLA_HEADS * (MLA_NOPE + MLA_V)), MLA_KV_RANK ** -0.5),
        p + 'w_out': _normal(k[6], (MLA_HEADS * MLA_V, D_MODEL), D_INNER ** -0.5),
    }


def setup_inputs(seed: int = 0) -> dict:
    key = jax.random.key(seed)
    keys = jax.random.split(key, DEPTH + 4)
    x = jax.random.normal(keys[0], (BATCH, SEQ, D_MODEL), jnp.float32)
    offset = jax.random.randint(keys[1], (BATCH, 1), 0, 1024, dtype=jnp.int32)
    positions = offset + jnp.arange(SEQ, dtype=jnp.int32)[None, :]
    inputs = {'x': x, 'positions': positions}
    makers = (_gmlp_params, _s5_params, _mla_params)
    for i in range(DEPTH):
        inputs.update(makers[i % N_MIXERS](keys[2 + i], 'l%d_' % i))
    inputs['final_norm_g'] = _gain(keys[2 + DEPTH], D_MODEL)
    return inputs


def reference(x, positions,
              l0_norm_g, l0_w_in, l0_ln_g, l0_ln_b, l0_w_s, l0_b_s, l0_w_out,
              l1_norm_g, l1_w_in, l1_a_re, l1_a_im, l1_log_step, l1_b_re, l1_b_im, l1_c_re, l1_c_im,
              l1_d_skip, l1_w_glu, l1_b_glu, l1_w_out,
              l2_norm_g, l2_w_in, l2_q_norm_g, l2_w_uq, l2_kv_norm_g, l2_w_ukv, l2_w_out,
              l3_norm_g, l3_w_in, l3_ln_g, l3_ln_b, l3_w_s, l3_b_s, l3_w_out,
              final_norm_g):
    layer_params = (
        (l0_norm_g, (l0_w_in, l0_ln_g, l0_ln_b, l0_w_s, l0_b_s, l0_w_out)),
        (l1_norm_g, (l1_w_in, l1_a_re, l1_a_im, l1_log_step, l1_b_re, l1_b_im, l1_c_re, l1_c_im,
                     l1_d_skip, l1_w_glu, l1_b_glu, l1_w_out)),
        (l2_norm_g, (l2_w_in, l2_q_norm_g, l2_w_uq, l2_kv_norm_g, l2_w_ukv, l2_w_out)),
        (l3_norm_g, (l3_w_in, l3_ln_g, l3_ln_b, l3_w_s, l3_b_s, l3_w_out)),
    )
    h = x
    for i in range(DEPTH):
        norm_g, p = layer_params[i]
        hn = _rmsnorm(h, norm_g)
        kind = i % N_MIXERS
        if kind == 0:
            y = _gmlp_mixer(hn, *p)
        elif kind == 1:
            y = _s5_mixer(hn, *p)
        else:
            y = _mla_mixer(hn, positions, *p)
        h = h + y
    return _rmsnorm(h, final_norm_g)
```

```python
import functools
import math

import jax
import jax.numpy as jnp
import numpy as np
from jax import lax
from jax.experimental import pallas as pl
from jax.experimental.pallas import tpu as pltpu

F32 = jnp.float32
BF16 = jnp.bfloat16

NORM_EPS = 1e-6
LANES = 128
VMEM_LIMIT = 56 * 1024 * 1024

GMLP_CHUNK = 128
GMLP_GROUPS = 8
S5_GROUP = 16
S5_STATE = 64
S5_Q = 32
MLA_HEADS = 16
MLA_NOPE = 128
MLA_ROPE = 64
MLA_V = 128
MLA_QK_PAD = 256
ROPE_THETA = 10000.0
NEG_INF = -1e30

ROW_BLOCK = 256


def _cparams(*sem):
    return pltpu.CompilerParams(dimension_semantics=sem, vmem_limit_bytes=VMEM_LIMIT)


def _const_spec(shape):
    nd = len(shape)
    return pl.BlockSpec(shape, lambda *_: (0,) * nd, pipeline_mode=pl.Buffered(1))


def _dot(a, b):
    return jnp.dot(a, b, preferred_element_type=F32)


def _dot_nt(a, b, precision=None):
    return lax.dot_general(a, b, (((1,), (1,)), ((), ())), preferred_element_type=F32, precision=precision)


def _rms(x, g):
    return x * lax.rsqrt(jnp.mean(x * x, axis=-1, keepdims=True) + NORM_EPS) * g


def _gelu(x):
    return 0.5 * x * (1.0 + jnp.tanh(math.sqrt(2.0 / math.pi) * (x + 0.044715 * (x * x * x))))


def _sigmoid(x):
    return 0.5 * (1.0 + jnp.tanh(0.5 * x))


def _silu(x):
    return x * _sigmoid(x)


def _gmlp_kernel(h_ref, ng_ref, win_ref, lng_ref, lnb_ref, ws_ref, bst_ref, wout_ref, *rest, d_inner, final):
    if final:
        fg_ref, o_ref = rest
    else:
        (o_ref,) = rest
    x = h_ref[...]
    tm = x.shape[0]
    hn = _rms(x, ng_ref[...]).astype(BF16)
    u = _gelu(_dot(hn, win_ref[:, 0:d_inner]))
    v = _gelu(_dot(hn, win_ref[:, d_inner:2 * d_inner]))
    mu = jnp.mean(v, axis=-1, keepdims=True)
    vc = v - mu
    var = jnp.mean(vc * vc, axis=-1, keepdims=True)
    vn = (vc * lax.rsqrt(var + NORM_EPS) * lng_ref[...] + lnb_ref[...]).astype(BF16)
    z = _dot(hn, win_ref[:, 2 * d_inner:3 * d_inner])
    gate = u * _silu(z)

    gw = d_inner // GMLP_GROUPS
    r = lax.broadcasted_iota(jnp.int32, (GMLP_CHUNK, GMLP_CHUNK), 0)
    c = lax.broadcasted_iota(jnp.int32, (GMLP_CHUNK, GMLP_CHUNK), 1)
    causal = c <= r
    cols = []
    for g in range(GMLP_GROUPS):
        w = jnp.where(causal, ws_ref[g], 0.0).astype(BF16)
        bias = bst_ref[:, g:g + 1]
        rows = []
        for ci in range(tm // GMLP_CHUNK):
            vs = vn[ci * GMLP_CHUNK:(ci + 1) * GMLP_CHUNK, g * gw:(g + 1) * gw]
            rows.append(_dot(w, vs) + bias)
        cols.append(jnp.concatenate(rows, axis=0) if len(rows) > 1 else rows[0])
    s = jnp.concatenate(cols, axis=1)
    y = (gate * s).astype(BF16)
    out = x + _dot(y, wout_ref[...])
    if final:
        out = _rms(out, fg_ref[...])
    o_ref[...] = out


def _gmlp_layer(h, norm_g, w_in, ln_g, ln_b, w_s, b_s, w_out, final_g=None):
    t, d = h.shape
    d_inner = w_out.shape[0]
    tm = ROW_BLOCK
    final = final_g is not None
    args = [h, norm_g.reshape(1, d), w_in.astype(BF16), ln_g.reshape(1, d_inner), ln_b.reshape(1, d_inner),
            w_s, b_s.T, w_out.astype(BF16)]
    specs = [pl.BlockSpec((tm, d), lambda i: (i, 0)), _const_spec((1, d)), _const_spec(w_in.shape),
             _const_spec((1, d_inner)), _const_spec((1, d_inner)), _const_spec(w_s.shape),
             _const_spec((GMLP_CHUNK, GMLP_GROUPS)), _const_spec(w_out.shape)]
    if final:
        args.append(final_g.reshape(1, d))
        specs.append(_const_spec((1, d)))
    return pl.pallas_call(
        functools.partial(_gmlp_kernel, d_inner=d_inner, final=final),
        grid=(t // tm,),
        in_specs=specs,
        out_specs=pl.BlockSpec((tm, d), lambda i: (i, 0)),
        out_shape=jax.ShapeDtypeStruct((t, d), F32),
        compiler_params=_cparams("parallel"),
        name="gmlp_final" if final else "gmlp",
    )(*args)


def _s5_in_kernel(h_ref, ng_ref, win_ref, u_ref, z_ref, *, d_inner):
    hn = _rms(h_ref[...], ng_ref[...]).astype(BF16)
    u_ref[...] = _dot(hn, win_ref[:, 0:d_inner]).astype(BF16)
    z_ref[...] = _dot(hn, win_ref[:, d_inner:2 * d_inner]).astype(BF16)


def _s5_in(h, norm_g, w_in):
    t, d = h.shape
    d_inner = w_in.shape[1] // 2
    tm = ROW_BLOCK
    return pl.pallas_call(
        functools.partial(_s5_in_kernel, d_inner=d_inner),
        grid=(t // tm,),
        in_specs=[pl.BlockSpec((tm, d), lambda i: (i, 0)), _const_spec((1, d)), _const_spec(w_in.shape)],
        out_specs=[pl.BlockSpec((tm, d_inner), lambda i: (i, 0))] * 2,
        out_shape=[jax.ShapeDtypeStruct((t, d_inner), BF16)] * 2,
        compiler_params=_cparams("parallel"),
        name="s5_in",
    )(h, norm_g.reshape(1, d), w_in.astype(BF16))


S5_MAX_LEVELS = 8
S5_POW_ROWS = 40


def _s5_prep_kernel(are_ref, aim_ref, ls_ref, bx_ref, by_ref, cr_ref, ci_ref,
                    m_ref, p_ref, rt_ref, la_ref, lb_ref, a_sc):
    q, hh = S5_Q, S5_GROUP
    lane = lax.broadcasted_iota(jnp.int32, (1, LANES), 1)
    lo = lane < S5_STATE
    sgn = jnp.where(lo, -1.0, 1.0).astype(F32)

    step = jnp.exp(ls_ref[0])
    ar, ai = are_ref[0], aim_ref[0]
    lr, li = ar * step, ai * step
    j = lax.broadcasted_iota(jnp.int32, (S5_POW_ROWS, LANES), 0).astype(F32)
    mag = jnp.exp(j * lr)
    er, ei = mag * jnp.cos(j * li), mag * jnp.sin(j * li)
    nr, ni = er[1:2] - 1.0, ei[1:2]
    den = 1.0 / (ar * ar + ai * ai)
    cfr = (nr * ar + ni * ai) * den
    cfi = (ni * ar - nr * ai) * den

    bx, by = bx_ref[0], by_ref[0]
    bba = cfr * bx + cfi * sgn * by
    bbb = sgn * cfr * by - cfi * bx
    bbt = -(sgn * cfr * bx) - cfi * by

    c1 = cr_ref[0]
    c2 = sgn * ci_ref[0]
    ee1 = jnp.where(lo, er, ei)
    ee2 = jnp.where(lo, ei, er)
    for jj in range(q + 1):
        a_sc[jj * hh:(jj + 1) * hh, :] = c1 * ee1[jj:jj + 1] + c2 * ee2[jj:jj + 1]

    for s in range(q):
        e = q - 1 - s
        p_ref[0, s * hh:(s + 1) * hh, :] = (er[e:e + 1] * bba + ei[e:e + 1] * bbb).astype(BF16)

    rt_ref[0] = (-sgn * a_sc[hh:(q + 1) * hh, :]).astype(BF16)

    kt = _dot_nt(bbt, a_sc[0:q * hh, :], precision=lax.Precision.HIGHEST)
    col = lax.broadcasted_iota(jnp.int32, kt.shape, 1)
    for s in range(q):
        if s == 0:
            blk = kt
        else:
            blk = jnp.where(col >= s * hh, pltpu.roll(kt, s * hh, axis=1), 0.0)
        m_ref[0, s * hh:(s + 1) * hh, :] = blk.astype(BF16)

    pr, pi = er[q:q + 1], ei[q:q + 1]
    for k in range(S5_MAX_LEVELS):
        la_ref[0, k:k + 1, :] = pr
        lb_ref[0, k:k + 1, :] = sgn * pi
        pr, pi = pr * pr - pi * pi, 2.0 * pr * pi


def _s5_prep(a_re, a_im, log_step, b_re, b_im, c_re, c_im):
    g, p = a_re.shape
    hh = b_re.shape[2]
    qh = S5_Q * hh
    dup = lambda x: jnp.concatenate([x, x], axis=-1)
    are2 = dup(a_re).reshape(g, 1, 2 * p)
    aim2 = dup(a_im).reshape(g, 1, 2 * p)
    ls = log_step.reshape(g, 1, 1)
    brt, bit = jnp.swapaxes(b_re, 1, 2), jnp.swapaxes(b_im, 1, 2)
    bx = jnp.concatenate([brt, bit], axis=-1)
    by = jnp.concatenate([bit, brt], axis=-1)
    gspec = lambda *s: pl.BlockSpec((1,) + s, lambda i: (i,) + (0,) * len(s))
    return pl.pallas_call(
        _s5_prep_kernel,
        grid=(g,),
        in_specs=[gspec(1, 2 * p), gspec(1, 2 * p), gspec(1, 1), gspec(hh, 2 * p), gspec(hh, 2 * p),
                  gspec(hh, 2 * p), gspec(hh, 2 * p)],
        out_specs=[gspec(qh, qh), gspec(qh, 2 * p), gspec(qh, 2 * p), gspec(8, 2 * p), gspec(8, 2 * p)],
        out_shape=[jax.ShapeDtypeStruct((g, qh, qh), BF16), jax.ShapeDtypeStruct((g, qh, 2 * p), BF16),
                   jax.ShapeDtypeStruct((g, qh, 2 * p), BF16), jax.ShapeDtypeStruct((g, 8, 2 * p), F32),
                   jax.ShapeDtypeStruct((g, 8, 2 * p), F32)],
        scratch_shapes=[pltpu.VMEM(((S5_Q + 1) * hh, 2 * p), F32)],
        compiler_params=_cparams("parallel"),
        name="s5_prep",
    )(are2, aim2, ls, bx, by, dup(c_re), dup(c_im))


def _s5_core_kernel(u_ref, m_ref, p_ref, rt_ref, la_ref, lb_ref, d_ref, y_ref, *, batch, levels):
    u = u_ref[0]
    z = _dot(u, p_ref[0])
    row = lax.broadcasted_iota(jnp.int32, z.shape, 0)
    la, lb = la_ref[0], lb_ref[0]
    for k in range(levels):
        sh = batch << k
        zs = jnp.where(row >= sh, pltpu.roll(z, sh, axis=0), 0.0)
        zw = pltpu.roll(zs, S5_STATE, axis=1)
        z = z + la[k:k + 1] * zs + lb[k:k + 1] * zw
    x = jnp.where(row >= batch, pltpu.roll(z, batch, axis=0), 0.0)
    y = _dot(u, m_ref[0]) + _dot_nt(x.astype(BF16), rt_ref[0]) + u.astype(F32) * d_ref[0]
    y_ref[0] = _gelu(y).astype(BF16)


def _s5_core(ur, m, p, rt, la, lb, dt, batch):
    g, n, qh = ur.shape
    levels = (n // batch - 1).bit_length()
    assert n // batch == 1 << levels and levels <= S5_MAX_LEVELS
    gspec = lambda *s: pl.BlockSpec((1,) + s, lambda i: (i,) + (0,) * len(s))
    return pl.pallas_call(
        functools.partial(_s5_core_kernel, batch=batch, levels=levels),
        grid=(g,),
        in_specs=[gspec(n, qh), gspec(qh, qh), gspec(qh, LANES), gspec(qh, LANES), gspec(8, LANES),
                  gspec(8, LANES), gspec(1, qh)],
        out_specs=gspec(n, qh),
        out_shape=jax.ShapeDtypeStruct((g, n, qh), BF16),
        compiler_params=_cparams("parallel"),
        name="s5_core",
    )(ur, m, p, rt, la, lb, dt)


def _s5_out_kernel(h_ref, y_ref, z_ref, wg_ref, bg_ref, wo_ref, o_ref):
    y = y_ref[...]
    g = _dot(y, wg_ref[...]) + bg_ref[...]
    y2 = y.astype(F32) * _sigmoid(g) * _silu(z_ref[...].astype(F32))
    o_ref[...] = h_ref[...] + _dot(y2.astype(BF16), wo_ref[...])


def _s5_out(h, y, z, w_glu, b_glu, w_out):
    t, d = h.shape
    d_inner = w_out.shape[0]
    tm = ROW_BLOCK
    return pl.pallas_call(
        _s5_out_kernel,
        grid=(t // tm,),
        in_specs=[pl.BlockSpec((tm, d), lambda i: (i, 0)), pl.BlockSpec((tm, d_inner), lambda i: (i, 0)),
                  pl.BlockSpec((tm, d_inner), lambda i: (i, 0)), _const_spec(w_glu.shape),
                  _const_spec((1, d_inner)), _const_spec(w_out.shape)],
        out_specs=pl.BlockSpec((tm, d), lambda i: (i, 0)),
        out_shape=jax.ShapeDtypeStruct((t, d), F32),
        compiler_params=_cparams("parallel"),
        name="s5_out",
    )(h, y, z, w_glu.astype(BF16), b_glu.reshape(1, d_inner), w_out.astype(BF16))


def _s5_layer(h, bsz, norm_g, w_in, a_re, a_im, log_step, b_re, b_im, c_re, c_im, d_skip, w_glu, b_glu, w_out):
    t, _ = h.shape
    seq = t // bsz
    g, _ = a_re.shape
    hh = b_re.shape[2]
    q = S5_Q
    nc = seq // q
    u, z = _s5_in(h, norm_g, w_in)
    ur = u.reshape(bsz, nc, q, g, hh).transpose(3, 1, 0, 2, 4).reshape(g, nc * bsz, q * hh)
    m, p, rt, la, lb = _s5_prep(a_re, a_im, log_step, b_re, b_im, c_re, c_im)
    dt = jnp.tile(d_skip.reshape(g, 1, hh), (1, 1, q))
    yr = _s5_core(ur, m, p, rt, la, lb, dt, bsz)
    y = yr.reshape(g, nc, bsz, q, hh).transpose(2, 1, 3, 0, 4).reshape(t, g * hh)
    return _s5_out(h, y, z, w_glu, b_glu, w_out)


def _rope_table_kernel(pos_ref, e_ref, cos_ref, sin_ref):
    p = pos_ref[...].astype(F32)
    ang = p[:, 0:1] * e_ref[0:1, :]
    for k in range(1, 4):
        ang = ang + p[:, k:k + 1] * e_ref[k:k + 1, :]
    cos_ref[...] = jnp.cos(ang)
    sin_ref[...] = jnp.sin(ang)


def _rope_tables(positions):
    t = positions.size
    half = MLA_ROPE // 2
    per_row = LANES // half
    inv_freq = (ROPE_THETA ** (-np.arange(0, MLA_ROPE, 2, dtype=np.float32) / np.float32(MLA_ROPE))).astype(np.float32)
    e = np.zeros((per_row, LANES), np.float32)
    for k in range(per_row):
        e[k, k * half:(k + 1) * half] = inv_freq
    rows = t // per_row
    tr = 512
    cos4, sin4 = pl.pallas_call(
        _rope_table_kernel,
        grid=(rows // tr,),
        in_specs=[pl.BlockSpec((tr, per_row), lambda i: (i, 0)), _const_spec((per_row, LANES))],
        out_specs=[pl.BlockSpec((tr, LANES), lambda i: (i, 0))] * 2,
        out_shape=[jax.ShapeDtypeStruct((rows, LANES), F32)] * 2,
        compiler_params=_cparams("parallel"),
        name="rope_table",
    )(positions.reshape(rows, per_row), jnp.asarray(e))
    return cos4.reshape(t, half), sin4.reshape(t, half)


def _rope128(x, c, s1, s2):
    return x * c - pltpu.roll(x, LANES - MLA_ROPE // 2, axis=1) * s1 + pltpu.roll(x, MLA_ROPE // 2, axis=1) * s2


def _mla_proj_kernel(h_ref, ng_ref, wa_ref, wz_ref, qg_ref, wuq_ref, kg_ref, wukv_ref, c_ref, s1_ref, s2_ref,
                     z_ref, q_ref, k_ref, v_ref, *, q_rank, kv_rank, scale):
    hn = _rms(h_ref[...], ng_ref[...]).astype(BF16)
    z_ref[...] = _dot(hn, wz_ref[...]).astype(BF16)
    a = _dot(hn, wa_ref[...])
    c, s1, s2 = c_ref[...], s1_ref[...], s2_ref[...]
    cq = _rms(a[:, 0:q_rank], qg_ref[...]).astype(BF16)
    ckv = _rms(a[:, q_rank:q_rank + kv_rank], kg_ref[...]).astype(BF16)
    kr = _rope128(a[:, q_rank + kv_rank:q_rank + kv_rank + LANES], c, s1, s2).astype(BF16)
    q = _dot(cq, wuq_ref[...]) * scale
    kv = _dot(ckv, wukv_ref[...])
    for hd in range(MLA_HEADS):
        o = hd * MLA_QK_PAD
        q_ref[:, o:o + MLA_NOPE] = q[:, o:o + MLA_NOPE].astype(BF16)
        q_ref[:, o + MLA_NOPE:o + MLA_QK_PAD] = _rope128(q[:, o + MLA_NOPE:o + MLA_QK_PAD], c, s1, s2).astype(BF16)
        k_ref[:, o:o + MLA_NOPE] = kv[:, o:o + MLA_NOPE].astype(BF16)
        k_ref[:, o + MLA_NOPE:o + MLA_QK_PAD] = kr
        v_ref[:, hd * MLA_V:(hd + 1) * MLA_V] = kv[:, o + MLA_NOPE:o + MLA_NOPE + MLA_V].astype(BF16)


def _mla_proj(h, norm_g, w_in, q_norm_g, w_uq, kv_norm_g, w_ukv, cos, sin):
    t, d = h.shape
    q_rank, kv_rank = w_uq.shape[0], w_ukv.shape[0]
    n_a = q_rank + kv_rank + MLA_ROPE
    d_inner = w_in.shape[1] - n_a
    tm = ROW_BLOCK
    wa = jnp.pad(w_in[:, :n_a], ((0, 0), (0, LANES - MLA_ROPE))).astype(BF16)
    wz = w_in[:, n_a:].astype(BF16)
    qk = MLA_NOPE + MLA_ROPE
    wuq = jnp.pad(w_uq.reshape(q_rank, MLA_HEADS, qk), ((0, 0), (0, 0), (0, MLA_QK_PAD - qk)))
    wuq = wuq.reshape(q_rank, MLA_HEADS * MLA_QK_PAD).astype(BF16)
    half = MLA_ROPE // 2
    zero = jnp.zeros_like(cos)
    ctab = jnp.concatenate([cos, cos, zero, zero], axis=1)
    s1tab = jnp.concatenate([sin, zero, zero, zero], axis=1)
    s2tab = jnp.concatenate([zero, sin, zero, zero], axis=1)
    assert ctab.shape[1] == LANES and half * 4 == LANES
    row = lambda n: pl.BlockSpec((tm, n), lambda i: (i, 0))
    return pl.pallas_call(
        functools.partial(_mla_proj_kernel, q_rank=q_rank, kv_rank=kv_rank, scale=float(qk) ** -0.5),
        grid=(t // tm,),
        in_specs=[row(d), _const_spec((1, d)), _const_spec(wa.shape), _const_spec(wz.shape),
                  _const_spec((1, q_rank)), _const_spec(wuq.shape), _const_spec((1, kv_rank)),
                  _const_spec(w_ukv.shape), row(LANES), row(LANES), row(LANES)],
        out_specs=[row(d_inner), row(MLA_HEADS * MLA_QK_PAD), row(MLA_HEADS * MLA_QK_PAD), row(MLA_HEADS * MLA_V)],
        out_shape=[jax.ShapeDtypeStruct((t, d_inner), BF16),
                   jax.ShapeDtypeStruct((t, MLA_HEADS * MLA_QK_PAD), BF16),
                   jax.ShapeDtypeStruct((t, MLA_HEADS * MLA_QK_PAD), BF16),
                   jax.ShapeDtypeStruct((t, MLA_HEADS * MLA_V), BF16)],
        compiler_params=_cparams("parallel"),
        name="mla_proj",
    )(h, norm_g.reshape(1, d), wa, wz, q_norm_g.reshape(1, q_rank), wuq, kv_norm_g.reshape(1, kv_rank),
      w_ukv.astype(BF16), ctab, s1tab, s2tab)


ATTN_TQ = 512
ATTN_TK = 512


def _attn_kernel(q_ref, k_ref, v_ref, o_ref, m_sc, l_sc, acc_sc):
    qi, ki = pl.program_id(2), pl.program_id(3)

    @pl.when(ki == 0)
    def _():
        m_sc[...] = jnp.full_like(m_sc, NEG_INF)
        l_sc[...] = jnp.zeros_like(l_sc)
        acc_sc[...] = jnp.zeros_like(acc_sc)

    def step(masked):
        s = _dot_nt(q_ref[...], k_ref[...])
        if masked:
            r = lax.broadcasted_iota(jnp.int32, s.shape, 0)
            c = lax.broadcasted_iota(jnp.int32, s.shape, 1)
            s = jnp.where(c <= r, s, NEG_INF)
        m_prev = m_sc[...]
        m_new = jnp.maximum(m_prev, jnp.max(s, axis=-1, keepdims=True))
        alpha = jnp.exp(m_prev - m_new)
        p = jnp.exp(s - m_new)
        l_sc[...] = alpha * l_sc[...] + jnp.sum(p, axis=-1, keepdims=True)
        acc_sc[...] = alpha * acc_sc[...] + _dot(p.astype(BF16), v_ref[...])
        m_sc[...] = m_new

    @pl.when(ki < qi)
    def _():
        step(False)

    @pl.when(ki == qi)
    def _():
        step(True)
        o_ref[...] = (acc_sc[...] / l_sc[...]).astype(o_ref.dtype)


def _attention(q, k, v, bsz):
    t = q.shape[0]
    seq = t // bsz
    assert ATTN_TQ == ATTN_TK
    nq = seq // ATTN_TQ
    qmap = lambda b, h, i, j: (b * nq + i, h)
    kmap = lambda b, h, i, j: (b * nq + jnp.minimum(i, j), h)
    return pl.pallas_call(
        _attn_kernel,
        grid=(bsz, MLA_HEADS, nq, nq),
        in_specs=[pl.BlockSpec((ATTN_TQ, MLA_QK_PAD), qmap), pl.BlockSpec((ATTN_TK, MLA_QK_PAD), kmap),
                  pl.BlockSpec((ATTN_TK, MLA_V), kmap)],
        out_specs=pl.BlockSpec((ATTN_TQ, MLA_V), qmap),
        out_shape=jax.ShapeDtypeStruct((t, MLA_HEADS * MLA_V), BF16),
        scratch_shapes=[pltpu.VMEM((ATTN_TQ, 1), F32), pltpu.VMEM((ATTN_TQ, 1), F32),
                        pltpu.VMEM((ATTN_TQ, MLA_V), F32)],
        compiler_params=_cparams("parallel", "parallel", "parallel", "arbitrary"),
        name="mla_attn",
    )(q, k, v)


def _mla_out_kernel(h_ref, o_ref, z_ref, wo_ref, out_ref):
    y = o_ref[...].astype(F32) * _silu(z_ref[...].astype(F32))
    out_ref[...] = h_ref[...] + _dot(y.astype(BF16), wo_ref[...])


def _mla_out(h, o, z, w_out):
    t, d = h.shape
    d_inner = w_out.shape[0]
    tm = ROW_BLOCK
    return pl.pallas_call(
        _mla_out_kernel,
        grid=(t // tm,),
        in_specs=[pl.BlockSpec((tm, d), lambda i: (i, 0)), pl.BlockSpec((tm, d_inner), lambda i: (i, 0)),
                  pl.BlockSpec((tm, d_inner), lambda i: (i, 0)), _const_spec(w_out.shape)],
        out_specs=pl.BlockSpec((tm, d), lambda i: (i, 0)),
        out_shape=jax.ShapeDtypeStruct((t, d), F32),
        compiler_params=_cparams("parallel"),
        name="mla_out",
    )(h, o, z, w_out.astype(BF16))


def _mla_layer(h, bsz, positions, norm_g, w_in, q_norm_g, w_uq, kv_norm_g, w_ukv, w_out):
    cos, sin = _rope_tables(positions)
    z, q, k, v = _mla_proj(h, norm_g, w_in, q_norm_g, w_uq, kv_norm_g, w_ukv, cos, sin)
    o = _attention(q, k, v, bsz)
    return _mla_out(h, o, z, w_out)


def kernel(x, positions, l0_norm_g, l0_w_in, l0_ln_g, l0_ln_b, l0_w_s, l0_b_s, l0_w_out, l1_norm_g, l1_w_in, l1_a_re, l1_a_im, l1_log_step, l1_b_re, l1_b_im, l1_c_re, l1_c_im, l1_d_skip, l1_w_glu, l1_b_glu, l1_w_out, l2_norm_g, l2_w_in, l2_q_norm_g, l2_w_uq, l2_kv_norm_g, l2_w_ukv, l2_w_out, l3_norm_g, l3_w_in, l3_ln_g, l3_ln_b, l3_w_s, l3_b_s, l3_w_out, final_norm_g):
    bsz, seq, d = x.shape
    h = x.reshape(bsz * seq, d)
    h = _gmlp_layer(h, l0_norm_g, l0_w_in, l0_ln_g, l0_ln_b, l0_w_s, l0_b_s, l0_w_out)
    h = _s5_layer(h, bsz, l1_norm_g, l1_w_in, l1_a_re, l1_a_im, l1_log_step, l1_b_re, l1_b_im, l1_c_re, l1_c_im,
                  l1_d_skip, l1_w_glu, l1_b_glu, l1_w_out)
    h = _mla_layer(h, bsz, positions, l2_norm_g, l2_w_in, l2_q_norm_g, l2_w_uq, l2_kv_norm_g, l2_w_ukv, l2_w_out)
    h = _gmlp_layer(h, l3_norm_g, l3_w_in, l3_ln_g, l3_ln_b, l3_w_s, l3_b_s, l3_w_out, final_g=final_norm_g)
    return h.reshape(bsz, seq, d)
```

```python
import functools
import math

import jax
import jax.numpy as jnp
import numpy as np
from jax import lax
from jax.experimental import pallas as pl
from jax.experimental.pallas import tpu as pltpu

F32 = jnp.float32
BF16 = jnp.bfloat16

NORM_EPS = 1e-6
LANES = 128
VMEM_LIMIT = 56 * 1024 * 1024

GMLP_CHUNK = 128
GMLP_GROUPS = 8
S5_GROUP = 16
S5_STATE = 64
S5_Q = 32
MLA_HEADS = 16
MLA_NOPE = 128
MLA_ROPE = 64
MLA_V = 128
MLA_QK_PAD = 256
ROPE_THETA = 10000.0
NEG_INF = -1e30

ROW_BLOCK = 256


def _cparams(*sem):
    return pltpu.CompilerParams(dimension_semantics=sem, vmem_limit_bytes=VMEM_LIMIT)


def _const_spec(shape):
    nd = len(shape)
    return pl.BlockSpec(shape, lambda *_: (0,) * nd, pipeline_mode=pl.Buffered(1))


def _dot(a, b):
    return jnp.dot(a, b, preferred_element_type=F32)


def _dot_nt(a, b, precision=None):
    return lax.dot_general(a, b, (((1,), (1,)), ((), ())), preferred_element_type=F32, precision=precision)


def _rms(x, g):
    return x * lax.rsqrt(jnp.mean(x * x, axis=-1, keepdims=True) + NORM_EPS) * g


def _gelu(x):
    return 0.5 * x * (1.0 + jnp.tanh(math.sqrt(2.0 / math.pi) * (x + 0.044715 * (x * x * x))))


def _sigmoid(x):
    return 0.5 * (1.0 + jnp.tanh(0.5 * x))


def _silu(x):
    return x * _sigmoid(x)


def _gmlp_kernel(h_ref, ng_ref, win_ref, lng_ref, lnb_ref, ws_ref, bst_ref, wout_ref, *rest, d_inner, final):
    if final:
        fg_ref, o_ref = rest
    else:
        (o_ref,) = rest
    x = h_ref[...]
    tm = x.shape[0]
    hn = _rms(x, ng_ref[...]).astype(BF16)
    u = _gelu(_dot(hn, win_ref[:, 0:d_inner]))
    v = _gelu(_dot(hn, win_ref[:, d_inner:2 * d_inner]))
    mu = jnp.mean(v, axis=-1, keepdims=True)
    vc = v - mu
    var = jnp.mean(vc * vc, axis=-1, keepdims=True)
    vn = (vc * lax.rsqrt(var + NORM_EPS) * lng_ref[...] + lnb_ref[...]).astype(BF16)
    z = _dot(hn, win_ref[:, 2 * d_inner:3 * d_inner])
    gate = u * _silu(z)

    gw = d_inner // GMLP_GROUPS
    r = lax.broadcasted_iota(jnp.int32, (GMLP_CHUNK, GMLP_CHUNK), 0)
    c = lax.broadcasted_iota(jnp.int32, (GMLP_CHUNK, GMLP_CHUNK), 1)
    causal = c <= r
    cols = []
    for g in range(GMLP_GROUPS):
        w = jnp.where(causal, ws_ref[g], 0.0).astype(BF16)
        bias = bst_ref[:, g:g + 1]
        rows = []
        for ci in range(tm // GMLP_CHUNK):
            vs = vn[ci * GMLP_CHUNK:(ci + 1) * GMLP_CHUNK, g * gw:(g + 1) * gw]
            rows.append(_dot(w, vs) + bias)
        cols.append(jnp.concatenate(rows, axis=0) if len(rows) > 1 else rows[0])
    s = jnp.concatenate(cols, axis=1)
    y = (gate * s).astype(BF16)
    out = x + _dot(y, wout_ref[...])
    if final:
        out = _rms(out, fg_ref[...])
    o_ref[...] = out


def _gmlp_layer(h, norm_g, w_in, ln_g, ln_b, w_s, b_s, w_out, final_g=None):
    t, d = h.shape
    d_inner = w_out.shape[0]
    tm = ROW_BLOCK
    final = final_g is not None
    args = [h, norm_g.reshape(1, d), w_in.astype(BF16), ln_g.reshape(1, d_inner), ln_b.reshape(1, d_inner),
            w_s, b_s.T, w_out.astype(BF16)]
    specs = [pl.BlockSpec((tm, d), lambda i: (i, 0)), _const_spec((1, d)), _const_spec(w_in.shape),
             _const_spec((1, d_inner)), _const_spec((1, d_inner)), _const_spec(w_s.shape),
             _const_spec((GMLP_CHUNK, GMLP_GROUPS)), _const_spec(w_out.shape)]
    if final:
        args.append(final_g.reshape(1, d))
        specs.append(_const_spec((1, d)))
    return pl.pallas_call(
        functools.partial(_gmlp_kernel, d_inner=d_inner, final=final),
        grid=(t // tm,),
        in_specs=specs,
        out_specs=pl.BlockSpec((tm, d), lambda i: (i, 0)),
        out_shape=jax.ShapeDtypeStruct((t, d), F32),
        compiler_params=_cparams("parallel"),
        name="gmlp_final" if final else "gmlp",
    )(*args)


def _s5_in_kernel(h_ref, ng_ref, win_ref, u_ref, z_ref, *, d_inner):
    hn = _rms(h_ref[...], ng_ref[...]).astype(BF16)
    u_ref[...] = _dot(hn, win_ref[:, 0:d_inner]).astype(BF16)
    z_ref[...] = _dot(hn, win_ref[:, d_inner:2 * d_inner]).astype(BF16)


def _s5_in(h, norm_g, w_in):
    t, d = h.shape
    d_inner = w_in.shape[1] // 2
    tm = ROW_BLOCK
    return pl.pallas_call(
        functools.partial(_s5_in_kernel, d_inner=d_inner),
        grid=(t // tm,),
        in_specs=[pl.BlockSpec((tm, d), lambda i: (i, 0)), _const_spec((1, d)), _const_spec(w_in.shape)],
        out_specs=[pl.BlockSpec((tm, d_inner), lambda i: (i, 0))] * 2,
        out_shape=[jax.ShapeDtypeStruct((t, d_inner), BF16)] * 2,
        compiler_params=_cparams("parallel"),
        name="s5_in",
    )(h, norm_g.reshape(1, d), w_in.astype(BF16))


S5_MAX_LEVELS = 8
S5_POW_ROWS = 40


def _s5_prep_kernel(are_ref, aim_ref, ls_ref, bx_ref, by_ref, cr_ref, ci_ref,
                    m_ref, p_ref, rt_ref, la_ref, lb_ref, a_sc):
    q, hh = S5_Q, S5_GROUP
    lane = lax.broadcasted_iota(jnp.int32, (1, LANES), 1)
    lo = lane < S5_STATE
    sgn = jnp.where(lo, -1.0, 1.0).astype(F32)

    step = jnp.exp(ls_ref[0])
    ar, ai = are_ref[0], aim_ref[0]
    lr, li = ar * step, ai * step
    j = lax.broadcasted_iota(jnp.int32, (S5_POW_ROWS, LANES), 0).astype(F32)
    mag = jnp.exp(j * lr)
    er, ei = mag * jnp.cos(j * li), mag * jnp.sin(j * li)
    nr, ni = er[1:2] - 1.0, ei[1:2]
    den = 1.0 / (ar * ar + ai * ai)
    cfr = (nr * ar + ni * ai) * den
    cfi = (ni * ar - nr * ai) * den

    bx, by = bx_ref[0], by_ref[0]
    bba = cfr * bx + cfi * sgn * by
    bbb = sgn * cfr * by - cfi * bx
    bbt = -(sgn * cfr * bx) - cfi * by

    c1 = cr_ref[0]
    c2 = sgn * ci_ref[0]
    ee1 = jnp.where(lo, er, ei)
    ee2 = jnp.where(lo, ei, er)
    for jj in range(q + 1):
        a_sc[jj * hh:(jj + 1) * hh, :] = c1 * ee1[jj:jj + 1] + c2 * ee2[jj:jj + 1]

    for s in range(q):
        e = q - 1 - s
        p_ref[0, s * hh:(s + 1) * hh, :] = (er[e:e + 1] * bba + ei[e:e + 1] * bbb).astype(BF16)

    rt_ref[0] = (-sgn * a_sc[hh:(q + 1) * hh, :]).astype(BF16)

    kt = _dot_nt(bbt, a_sc[0:q * hh, :], precision=lax.Precision.HIGHEST)
    col = lax.broadcasted_iota(jnp.int32, kt.shape, 1)
    for s in range(q):
        if s == 0:
            blk = kt
        else:
            blk = jnp.where(col >= s * hh, pltpu.roll(kt, s * hh, axis=1), 0.0)
        m_ref[0, s * hh:(s + 1) * hh, :] = blk.astype(BF16)

    pr, pi = er[q:q + 1], ei[q:q + 1]
    for k in range(S5_MAX_LEVELS):
        la_ref[0, k:k + 1, :] = pr
        lb_ref[0, k:k + 1, :] = sgn * pi
        pr, pi = pr * pr - pi * pi, 2.0 * pr * pi


def _s5_prep(a_re, a_im, log_step, b_re, b_im, c_re, c_im):
    g, p = a_re.shape
    hh = b_re.shape[2]
    qh = S5_Q * hh
    dup = lambda x: jnp.concatenate([x, x], axis=-1)
    are2 = dup(a_re).reshape(g, 1, 2 * p)
    aim2 = dup(a_im).reshape(g, 1, 2 * p)
    ls = log_step.reshape(g, 1, 1)
    brt, bit = jnp.swapaxes(b_re, 1, 2), jnp.swapaxes(b_im, 1, 2)
    bx = jnp.concatenate([brt, bit], axis=-1)
    by = jnp.concatenate([bit, brt], axis=-1)
    gspec = lambda *s: pl.BlockSpec((1,) + s, lambda i: (i,) + (0,) * len(s))
    return pl.pallas_call(
        _s5_prep_kernel,
        grid=(g,),
        in_specs=[gspec(1, 2 * p), gspec(1, 2 * p), gspec(1, 1), gspec(hh, 2 * p), gspec(hh, 2 * p),
                  gspec(hh, 2 * p), gspec(hh, 2 * p)],
        out_specs=[gspec(qh, qh), gspec(qh, 2 * p), gspec(qh, 2 * p), gspec(8, 2 * p), gspec(8, 2 * p)],
        out_shape=[jax.ShapeDtypeStruct((g, qh, qh), BF16), jax.ShapeDtypeStruct((g, qh, 2 * p), BF16),
                   jax.ShapeDtypeStruct((g, qh, 2 * p), BF16), jax.ShapeDtypeStruct((g, 8, 2 * p), F32),
                   jax.ShapeDtypeStruct((g, 8, 2 * p), F32)],
        scratch_shapes=[pltpu.VMEM(((S5_Q + 1) * hh, 2 * p), F32)],
        compiler_params=_cparams("parallel"),
        name="s5_prep",
    )(are2, aim2, ls, bx, by, dup(c_re), dup(c_im))


def _s5_core_kernel(u_ref, m_ref, p_ref, rt_ref, la_ref, lb_ref, d_ref, y_ref, *, batch, levels):
    u = u_ref[0]
    z = _dot(u, p_ref[0])
    row = lax.broadcasted_iota(jnp.int32, z.shape, 0)
    la, lb = la_ref[0], lb_ref[0]
    for k in range(levels):
        sh = batch << k
        zs = jnp.where(row >= sh, pltpu.roll(z, sh, axis=0), 0.0)
        zw = pltpu.roll(zs, S5_STATE, axis=1)
        z = z + la[k:k + 1] * zs + lb[k:k + 1] * zw
    x = jnp.where(row >= batch, pltpu.roll(z, batch, axis=0), 0.0)
    y = _dot(u, m_ref[0]) + _dot_nt(x.astype(BF16), rt_ref[0]) + u.astype(F32) * d_ref[0]
    y_ref[0] = _gelu(y).astype(BF16)


def _s5_core(ur, m, p, rt, la, lb, dt, batch):
    g, n, qh = ur.shape
    levels = (n // batch - 1).bit_length()
    assert n // batch == 1 << levels and levels <= S5_MAX_LEVELS
    gspec = lambda *s: pl.BlockSpec((1,) + s, lambda i: (i,) + (0,) * len(s))
    return pl.pallas_call(
        functools.partial(_s5_core_kernel, batch=batch, levels=levels),
        grid=(g,),
        in_specs=[gspec(n, qh), gspec(qh, qh), gspec(qh, LANES), gspec(qh, LANES), gspec(8, LANES),
                  gspec(8, LANES), gspec(1, qh)],
        out_specs=gspec(n, qh),
        out_shape=jax.ShapeDtypeStruct((g, n, qh), BF16),
        compiler_params=_cparams("parallel"),
        name="s5_core",
    )(ur, m, p, rt, la, lb, dt)


def _s5_out_kernel(h_ref, y_ref, z_ref, wg_ref, bg_ref, wo_ref, o_ref):
    y = y_ref[...]
    g = _dot(y, wg_ref[...]) + bg_ref[...]
    y2 = y.astype(F32) * _sigmoid(g) * _silu(z_ref[...].astype(F32))
    o_ref[...] = h_ref[...] + _dot(y2.astype(BF16), wo_ref[...])


def _s5_out(h, y, z, w_glu, b_glu, w_out):
    t, d = h.shape
    d_inner = w_out.shape[0]
    tm = ROW_BLOCK
    return pl.pallas_call(
        _s5_out_kernel,
        grid=(t // tm,),
        in_specs=[pl.BlockSpec((tm, d), lambda i: (i, 0)), pl.BlockSpec((tm, d_inner), lambda i: (i, 0)),
                  pl.BlockSpec((tm, d_inner), lambda i: (i, 0)), _const_spec(w_glu.shape),
                  _const_spec((1, d_inner)), _const_spec(w_out.shape)],
        out_specs=pl.BlockSpec((tm, d), lambda i: (i, 0)),
        out_shape=jax.ShapeDtypeStruct((t, d), F32),
        compiler_params=_cparams("parallel"),
        name="s5_out",
    )(h, y, z, w_glu.astype(BF16), b_glu.reshape(1, d_inner), w_out.astype(BF16))


def _s5_layer(h, bsz, norm_g, w_in, a_re, a_im, log_step, b_re, b_im, c_re, c_im, d_skip, w_glu, b_glu, w_out):
    t, _ = h.shape
    seq = t // bsz
    g, _ = a_re.shape
    hh = b_re.shape[2]
    q = S5_Q
    nc = seq // q
    u, z = _s5_in(h, norm_g, w_in)
    ur = u.reshape(bsz, nc, q, g, hh).transpose(3, 1, 0, 2, 4).reshape(g, nc * bsz, q * hh)
    m, p, rt, la, lb = _s5_prep(a_re, a_im, log_step, b_re, b_im, c_re, c_im)
    dt = jnp.tile(d_skip.reshape(g, 1, hh), (1, 1, q))
    yr = _s5_core(ur, m, p, rt, la, lb, dt, bsz)
    y = yr.reshape(g, nc, bsz, q, hh).transpose(2, 1, 3, 0, 4).reshape(t, g * hh)
    return _s5_out(h, y, z, w_glu, b_glu, w_out)


def _rope_table_kernel(pos_ref, e_ref, cos_ref, sin_ref):
    p = pos_ref[...].astype(F32)
    ang = p[:, 0:1] * e_ref[0:1, :]
    for k in range(1, 4):
        ang = ang + p[:, k:k + 1] * e_ref[k:k + 1, :]
    cos_ref[...] = jnp.cos(ang)
    sin_ref[...] = jnp.sin(ang)


def _rope_tables(positions):
    t = positions.size
    half = MLA_ROPE // 2
    per_row = LANES // half
    inv_freq = (ROPE_THETA ** (-np.arange(0, MLA_ROPE, 2, dtype=np.float32) / np.float32(MLA_ROPE))).astype(np.float32)
    e = np.zeros((per_row, LANES), np.float32)
    for k in range(per_row):
        e[k, k * half:(k + 1) * half] = inv_freq
    rows = t // per_row
    tr = 512
    cos4, sin4 = pl.pallas_call(
        _rope_table_kernel,
        grid=(rows // tr,),
        in_specs=[pl.BlockSpec((tr, per_row), lambda i: (i, 0)), _const_spec((per_row, LANES))],
        out_specs=[pl.BlockSpec((tr, LANES), lambda i: (i, 0))] * 2,
        out_shape=[jax.ShapeDtypeStruct((rows, LANES), F32)] * 2,
        compiler_params=_cparams("parallel"),
        name="rope_table",
    )(positions.reshape(rows, per_row), jnp.asarray(e))
    return cos4.reshape(t, half), sin4.reshape(t, half)


def _rope128(x, c, s1, s2):
    return x * c - pltpu.roll(x, LANES - MLA_ROPE // 2, axis=1) * s1 + pltpu.roll(x, MLA_ROPE // 2, axis=1) * s2


def _mla_proj_kernel(h_ref, ng_ref, wa_ref, wz_ref, qg_ref, wuq_ref, kg_ref, wukv_ref, c_ref, s1_ref, s2_ref,
                     z_ref, q_ref, k_ref, v_ref, *, q_rank, kv_rank, scale):
    hn = _rms(h_ref[...], ng_ref[...]).astype(BF16)
    z_ref[...] = _dot(hn, wz_ref[...]).astype(BF16)
    a = _dot(hn, wa_ref[...])
    c, s1, s2 = c_ref[...], s1_ref[...], s2_ref[...]
    cq = _rms(a[:, 0:q_rank], qg_ref[...]).astype(BF16)
    ckv = _rms(a[:, q_rank:q_rank + kv_rank], kg_ref[...]).astype(BF16)
    kr = _rope128(a[:, q_rank + kv_rank:q_rank + kv_rank + LANES], c, s1, s2).astype(BF16)
    q = _dot(cq, wuq_ref[...]) * scale
    kv = _dot(ckv, wukv_ref[...])
    for hd in range(MLA_HEADS):
        o = hd * MLA_QK_PAD
        q_ref[:, o:o + MLA_NOPE] = q[:, o:o + MLA_NOPE].astype(BF16)
        q_ref[:, o + MLA_NOPE:o + MLA_QK_PAD] = _rope128(q[:, o + MLA_NOPE:o + MLA_QK_PAD], c, s1, s2).astype(BF16)
        k_ref[:, o:o + MLA_NOPE] = kv[:, o:o + MLA_NOPE].astype(BF16)
        k_ref[:, o + MLA_NOPE:o + MLA_QK_PAD] = kr
        v_ref[:, hd * MLA_V:(hd + 1) * MLA_V] = kv[:, o + MLA_NOPE:o + MLA_NOPE + MLA_V].astype(BF16)


def _mla_proj(h, norm_g, w_in, q_norm_g, w_uq, kv_norm_g, w_ukv, cos, sin):
    t, d = h.shape
    q_rank, kv_rank = w_uq.shape[0], w_ukv.shape[0]
    n_a = q_rank + kv_rank + MLA_ROPE
    d_inner = w_in.shape[1] - n_a
    tm = ROW_BLOCK
    wa = jnp.pad(w_in[:, :n_a], ((0, 0), (0, LANES - MLA_ROPE))).astype(BF16)
    wz = w_in[:, n_a:].astype(BF16)
    qk = MLA_NOPE + MLA_ROPE
    wuq = jnp.pad(w_uq.reshape(q_rank, MLA_HEADS, qk), ((0, 0), (0, 0), (0, MLA_QK_PAD - qk)))
    wuq = wuq.reshape(q_rank, MLA_HEADS * MLA_QK_PAD).astype(BF16)
    half = MLA_ROPE // 2
    zero = jnp.zeros_like(cos)
    ctab = jnp.concatenate([cos, cos, zero, zero], axis=1)
    s1tab = jnp.concatenate([sin, zero, zero, zero], axis=1)
    s2tab = jnp.concatenate([zero, sin, zero, zero], axis=1)
    assert ctab.shape[1] == LANES and half * 4 == LANES
    row = lambda n: pl.BlockSpec((tm, n), lambda i: (i, 0))
    return pl.pallas_call(
        functools.partial(_mla_proj_kernel, q_rank=q_rank, kv_rank=kv_rank, scale=float(qk) ** -0.5 * math.log2(math.e)),
        grid=(t // tm,),
        in_specs=[row(d), _const_spec((1, d)), _const_spec(wa.shape), _const_spec(wz.shape),
                  _const_spec((1, q_rank)), _const_spec(wuq.shape), _const_spec((1, kv_rank)),
                  _const_spec(w_ukv.shape), row(LANES), row(LANES), row(LANES)],
        out_specs=[row(d_inner), row(MLA_HEADS * MLA_QK_PAD), row(MLA_HEADS * MLA_QK_PAD), row(MLA_HEADS * MLA_V)],
        out_shape=[jax.ShapeDtypeStruct((t, d_inner), BF16),
                   jax.ShapeDtypeStruct((t, MLA_HEADS * MLA_QK_PAD), BF16),
                   jax.ShapeDtypeStruct((t, MLA_HEADS * MLA_QK_PAD), BF16),
                   jax.ShapeDtypeStruct((t, MLA_HEADS * MLA_V), BF16)],
        compiler_params=_cparams("parallel"),
        name="mla_proj",
    )(h, norm_g.reshape(1, d), wa, wz, q_norm_g.reshape(1, q_rank), wuq, kv_norm_g.reshape(1, kv_rank),
      w_ukv.astype(BF16), ctab, s1tab, s2tab)


ATTN_TQ = 1024
ATTN_TK = 512


def _attn_kernel_pipe(q_ref, k_ref, v_ref, o_ref, vx_sc, acc_sc, m_sc, p0_sc, p1_sc, al0_sc, al1_sc):
    qi = pl.program_id(2)
    tq, tk = ATTN_TQ, ATTN_TK
    p_sc, al_sc = (p0_sc, p1_sc), (al0_sc, al1_sc)

    @pl.when(qi == 0)
    def _():
        vx_sc[:, 0:MLA_V] = v_ref[...]
        lane = lax.broadcasted_iota(jnp.int32, (vx_sc.shape[0], LANES), 1)
        vx_sc[:, MLA_V:MLA_V + LANES] = jnp.where(lane == 0, 1.0, 0.0).astype(BF16)

    m_sc[...] = jnp.full_like(m_sc, NEG_INF)
    acc_sc[...] = jnp.zeros_like(acc_sc)

    def scores(koff, slot, mask_off=None):
        s = _dot_nt(q_ref[...], k_ref[pl.ds(koff, tk), :])
        if mask_off is not None:
            r = lax.broadcasted_iota(jnp.int32, s.shape, 0)
            c = lax.broadcasted_iota(jnp.int32, s.shape, 1) + mask_off
            s = jnp.where(c <= r, s, NEG_INF)
        m_prev = m_sc[...]
        m_new = jnp.maximum(m_prev, jnp.max(s, axis=-1, keepdims=True))
        al_sc[slot][...] = jnp.exp2(m_prev - m_new)
        p_sc[slot][...] = jnp.exp2(s - m_new).astype(BF16)
        m_sc[...] = m_new

    def accumulate(koff, slot):
        pv = _dot(p_sc[slot][...], vx_sc[pl.ds(koff, tk), :])
        acc_sc[...] = al_sc[slot][...] * acc_sc[...] + pv

    diag = pl.multiple_of(qi * tq, tq)
    scores(diag, 0, mask_off=0)
    accumulate(diag, 0)
    scores(diag + tk, 1, mask_off=tk)

    def body(i, carry):
        k0 = pl.multiple_of(2 * i * tk, tk)
        prev = jnp.where(i == 0, diag + tk, k0 - tk)
        accumulate(pl.multiple_of(prev, tk), 1)
        scores(k0, 0)
        accumulate(k0, 0)
        scores(k0 + tk, 1)
        return carry

    lax.fori_loop(0, qi, body, 0)
    last = jnp.where(qi == 0, diag + tk, diag - tk)
    accumulate(pl.multiple_of(last, tk), 1)
    acc = acc_sc[...]
    o_ref[...] = (acc[:, 0:MLA_V] / acc[:, MLA_V:MLA_V + 1]).astype(o_ref.dtype)


def _attention_pipe(q, k, v, bsz):
    t = q.shape[0]
    seq = t // bsz
    assert ATTN_TQ == 2 * ATTN_TK and seq % ATTN_TQ == 0
    nq = seq // ATTN_TQ
    qmap = lambda b, h, i: (b * nq + i, h)
    kmap = lambda b, h, i: (b, h)
    return pl.pallas_call(
        _attn_kernel,
        grid=(bsz, MLA_HEADS, nq),
        in_specs=[pl.BlockSpec((ATTN_TQ, MLA_QK_PAD), qmap), pl.BlockSpec((seq, MLA_QK_PAD), kmap),
                  pl.BlockSpec((seq, MLA_V), kmap)],
        out_specs=pl.BlockSpec((ATTN_TQ, MLA_V), qmap),
        out_shape=jax.ShapeDtypeStruct((t, MLA_HEADS * MLA_V), BF16),
        scratch_shapes=[pltpu.VMEM((seq, MLA_V + LANES), BF16), pltpu.VMEM((ATTN_TQ, MLA_V + LANES), F32),
                        pltpu.VMEM((ATTN_TQ, 1), F32),
                        pltpu.VMEM((ATTN_TQ, ATTN_TK), BF16), pltpu.VMEM((ATTN_TQ, ATTN_TK), BF16),
                        pltpu.VMEM((ATTN_TQ, 1), F32), pltpu.VMEM((ATTN_TQ, 1), F32)],
        compiler_params=_cparams("parallel", "parallel", "arbitrary"),
        name="mla_attn",
    )(q, k, v)


ATT_K = 512
ATT_CHAINS = 2
ATT_Q = ATT_CHAINS * ATT_K
ATT_VROWS = MLA_V + 16


def _attn_kernel(q_ref, k_ref, v_ref, o_ref, vxt_sc, acc_sc, m_sc, st0_sc, st1_sc, pt0_sc, pt1_sc, al0_sc, al1_sc):
    qi = pl.program_id(2)
    n_kt = vxt_sc.shape[0]

    @pl.when(qi == 0)
    def _():
        row = lax.broadcasted_iota(jnp.int32, (ATT_VROWS - MLA_V, ATT_K), 0)
        ones = jnp.where(row == 0, 1.0, 0.0).astype(BF16)
        for j in range(n_kt):
            vt = v_ref[j * ATT_K:(j + 1) * ATT_K, :].astype(F32).T
            vxt_sc[j, 0:MLA_V, :] = vt.astype(BF16)
            vxt_sc[j, MLA_V:ATT_VROWS, :] = ones

    m_sc[...] = jnp.full_like(m_sc, NEG_INF)
    acc_sc[...] = jnp.zeros_like(acc_sc)

    st_sc, pt_sc, al_sc = (st0_sc, st1_sc), (pt0_sc, pt1_sc), (al0_sc, al1_sc)

    def qk(c, j):
        koff = pl.multiple_of(j * ATT_K, ATT_K)
        st_sc[c][...] = _dot_nt(k_ref[pl.ds(koff, ATT_K), :], q_ref[c * ATT_K:(c + 1) * ATT_K, :])

    def softmax(c, masked):
        qs = slice(c * ATT_K, (c + 1) * ATT_K)
        st = st_sc[c][...]
        if masked:
            kk = lax.broadcasted_iota(jnp.int32, st.shape, 0)
            qq = lax.broadcasted_iota(jnp.int32, st.shape, 1)
            st = jnp.where(kk <= qq, st, NEG_INF)
        m_prev = m_sc[:, qs]
        m_new = jnp.maximum(m_prev, jnp.max(st, axis=0, keepdims=True))
        al_sc[c][...] = jnp.exp2(m_prev - m_new)
        pt_sc[c][...] = jnp.exp2(st - m_new).astype(BF16)
        m_sc[:, qs] = m_new

    def pv(c, j):
        qs = slice(c * ATT_K, (c + 1) * ATT_K)
        acc_sc[:, qs] = al_sc[c][...] * acc_sc[:, qs] + _dot(vxt_sc[j], pt_sc[c][...])

    base = qi * ATT_CHAINS
    qk(0, 0)

    def one_tile(j):
        qk(1, j)
        softmax(0, False)
        pv(0, j)
        softmax(1, False)
        qk(0, j + 1)
        pv(1, j)

    def body(i, carry):
        one_tile(2 * i)
        one_tile(2 * i + 1)
        return carry

    lax.fori_loop(0, qi, body, 0)
    qk(1, base)
    softmax(0, True)
    pv(0, base)
    softmax(1, False)
    qk(1, base + 1)
    pv(1, base)
    softmax(1, True)
    pv(1, base + 1)
    acc = acc_sc[...]
    o = acc[0:MLA_V, :] / acc[MLA_V:MLA_V + 1, :]
    o_ref[...] = o.T.astype(o_ref.dtype)


def _attention(q, k, v, bsz):
    t = q.shape[0]
    seq = t // bsz
    assert seq % ATT_Q == 0
    nq = seq // ATT_Q
    qmap = lambda b, h, i: (b * nq + i, h)
    kmap = lambda b, h, i: (b, h)
    return pl.pallas_call(
        _attn_kernel,
        grid=(bsz, MLA_HEADS, nq),
        in_specs=[pl.BlockSpec((ATT_Q, MLA_QK_PAD), qmap), pl.BlockSpec((seq, MLA_QK_PAD), kmap),
                  pl.BlockSpec((seq, MLA_V), kmap)],
        out_specs=pl.BlockSpec((ATT_Q, MLA_V), qmap),
        out_shape=jax.ShapeDtypeStruct((t, MLA_HEADS * MLA_V), BF16),
        scratch_shapes=[pltpu.VMEM((seq // ATT_K, ATT_VROWS, ATT_K), BF16), pltpu.VMEM((ATT_VROWS, ATT_Q), F32),
                        pltpu.VMEM((1, ATT_Q), F32),
                        pltpu.VMEM((ATT_K, ATT_K), F32), pltpu.VMEM((ATT_K, ATT_K), F32),
                        pltpu.VMEM((ATT_K, ATT_K), BF16), pltpu.VMEM((ATT_K, ATT_K), BF16),
                        pltpu.VMEM((1, ATT_K), F32), pltpu.VMEM((1, ATT_K), F32)],
        compiler_params=_cparams("parallel", "parallel", "arbitrary"),
        name="mla_attn",
    )(q, k, v)


def _mla_out_kernel(h_ref, o_ref, z_ref, wo_ref, out_ref):
    y = o_ref[...].astype(F32) * _silu(z_ref[...].astype(F32))
    out_ref[...] = h_ref[...] + _dot(y.astype(BF16), wo_ref[...])


def _mla_out(h, o, z, w_out):
    t, d = h.shape
    d_inner = w_out.shape[0]
    tm = ROW_BLOCK
    return pl.pallas_call(
        _mla_out_kernel,
        grid=(t // tm,),
        in_specs=[pl.BlockSpec((tm, d), lambda i: (i, 0)), pl.BlockSpec((tm, d_inner), lambda i: (i, 0)),
                  pl.BlockSpec((tm, d_inner), lambda i: (i, 0)), _const_spec(w_out.shape)],
        out_specs=pl.BlockSpec((tm, d), lambda i: (i, 0)),
        out_shape=jax.ShapeDtypeStruct((t, d), F32),
        compiler_params=_cparams("parallel"),
        name="mla_out",
    )(h, o, z, w_out.astype(BF16))


def _mla_layer(h, bsz, positions, norm_g, w_in, q_norm_g, w_uq, kv_norm_g, w_ukv, w_out):
    cos, sin = _rope_tables(positions)
    z, q, k, v = _mla_proj(h, norm_g, w_in, q_norm_g, w_uq, kv_norm_g, w_ukv, cos, sin)
    o = _attention(q, k, v, bsz)
    return _mla_out(h, o, z, w_out)


def kernel(x, positions, l0_norm_g, l0_w_in, l0_ln_g, l0_ln_b, l0_w_s, l0_b_s, l0_w_out, l1_norm_g, l1_w_in, l1_a_re, l1_a_im, l1_log_step, l1_b_re, l1_b_im, l1_c_re, l1_c_im, l1_d_skip, l1_w_glu, l1_b_glu, l1_w_out, l2_norm_g, l2_w_in, l2_q_norm_g, l2_w_uq, l2_kv_norm_g, l2_w_ukv, l2_w_out, l3_norm_g, l3_w_in, l3_ln_g, l3_ln_b, l3_w_s, l3_b_s, l3_w_out, final_norm_g):
    bsz, seq, d = x.shape
    h = x.reshape(bsz * seq, d)
    h = _gmlp_layer(h, l0_norm_g, l0_w_in, l0_ln_g, l0_ln_b, l0_w_s, l0_b_s, l0_w_out)
    h = _s5_layer(h, bsz, l1_norm_g, l1_w_in, l1_a_re, l1_a_im, l1_log_step, l1_b_re, l1_b_im, l1_c_re, l1_c_im,
                  l1_d_skip, l1_w_glu, l1_b_glu, l1_w_out)
    h = _mla_layer(h, bsz, positions, l2_norm_g, l2_w_in, l2_q_norm_g, l2_w_uq, l2_kv_norm_g, l2_w_ukv, l2_w_out)
    h = _gmlp_layer(h, l3_norm_g, l3_w_in, l3_ln_g, l3_ln_b, l3_w_s, l3_b_s, l3_w_out, final_g=final_norm_g)
    return h.reshape(bsz, seq, d)
```

```python
import functools
import math

import jax
import jax.numpy as jnp
import numpy as np
from jax import lax
from jax.experimental import pallas as pl
from jax.experimental.pallas import tpu as pltpu

F32 = jnp.float32
BF16 = jnp.bfloat16

NORM_EPS = 1e-6
LANES = 128
VMEM_LIMIT = 56 * 1024 * 1024

GMLP_CHUNK = 128
GMLP_GROUPS = 8
S5_GROUP = 16
S5_STATE = 64
S5_Q = 32
MLA_HEADS = 16
MLA_NOPE = 128
MLA_ROPE = 64
MLA_V = 128
MLA_QK_PAD = 256
ROPE_THETA = 10000.0
NEG_INF = -1e30

ROW_BLOCK = 512


def _cparams(*sem):
    return pltpu.CompilerParams(dimension_semantics=sem, vmem_limit_bytes=VMEM_LIMIT)


def _const_spec(shape):
    nd = len(shape)
    return pl.BlockSpec(shape, lambda *_: (0,) * nd, pipeline_mode=pl.Buffered(1))


def _dot(a, b):
    return jnp.dot(a, b, preferred_element_type=F32)


def _dot_nt(a, b, precision=None):
    return lax.dot_general(a, b, (((1,), (1,)), ((), ())), preferred_element_type=F32, precision=precision)


def _rms(x, g):
    return x * lax.rsqrt(jnp.mean(x * x, axis=-1, keepdims=True) + NORM_EPS) * g


def _gelu(x):
    return 0.5 * x * (1.0 + jnp.tanh(math.sqrt(2.0 / math.pi) * (x + 0.044715 * (x * x * x))))


def _sigmoid(x):
    return 0.5 * (1.0 + jnp.tanh(0.5 * x))


def _silu(x):
    return x * _sigmoid(x)


def _gmlp_kernel(h_ref, ng_ref, win_ref, lng_ref, lnb_ref, ws_ref, bst_ref, wout_ref, *rest, d_inner, final):
    if final:
        fg_ref, o_ref = rest
    else:
        (o_ref,) = rest
    x = h_ref[...]
    tm = x.shape[0]
    hn = _rms(x, ng_ref[...]).astype(BF16)
    u = _gelu(_dot(hn, win_ref[:, 0:d_inner]))
    v = _gelu(_dot(hn, win_ref[:, d_inner:2 * d_inner]))
    mu = jnp.mean(v, axis=-1, keepdims=True)
    vc = v - mu
    var = jnp.mean(vc * vc, axis=-1, keepdims=True)
    vn = (vc * lax.rsqrt(var + NORM_EPS) * lng_ref[...] + lnb_ref[...]).astype(BF16)
    z = _dot(hn, win_ref[:, 2 * d_inner:3 * d_inner])
    gate = u * _silu(z)

    gw = d_inner // GMLP_GROUPS
    r = lax.broadcasted_iota(jnp.int32, (GMLP_CHUNK, GMLP_CHUNK), 0)
    c = lax.broadcasted_iota(jnp.int32, (GMLP_CHUNK, GMLP_CHUNK), 1)
    causal = c <= r
    cols = []
    for g in range(GMLP_GROUPS):
        w = jnp.where(causal, ws_ref[g], 0.0).astype(BF16)
        bias = bst_ref[:, g:g + 1]
        rows = []
        for ci in range(tm // GMLP_CHUNK):
            vs = vn[ci * GMLP_CHUNK:(ci + 1) * GMLP_CHUNK, g * gw:(g + 1) * gw]
            rows.append(_dot(w, vs) + bias)
        cols.append(jnp.concatenate(rows, axis=0) if len(rows) > 1 else rows[0])
    s = jnp.concatenate(cols, axis=1)
    y = (gate * s).astype(BF16)
    out = x + _dot(y, wout_ref[...])
    if final:
        out = _rms(out, fg_ref[...])
    o_ref[...] = out


def _gmlp_layer(h, norm_g, w_in, ln_g, ln_b, w_s, b_s, w_out, final_g=None):
    t, d = h.shape
    d_inner = w_out.shape[0]
    tm = ROW_BLOCK
    final = final_g is not None
    args = [h, norm_g.reshape(1, d), w_in.astype(BF16), ln_g.reshape(1, d_inner), ln_b.reshape(1, d_inner),
            w_s, b_s.T, w_out.astype(BF16)]
    specs = [pl.BlockSpec((tm, d), lambda i: (i, 0)), _const_spec((1, d)), _const_spec(w_in.shape),
             _const_spec((1, d_inner)), _const_spec((1, d_inner)), _const_spec(w_s.shape),
             _const_spec((GMLP_CHUNK, GMLP_GROUPS)), _const_spec(w_out.shape)]
    if final:
        args.append(final_g.reshape(1, d))
        specs.append(_const_spec((1, d)))
    return pl.pallas_call(
        functools.partial(_gmlp_kernel, d_inner=d_inner, final=final),
        grid=(t // tm,),
        in_specs=specs,
        out_specs=pl.BlockSpec((tm, d), lambda i: (i, 0)),
        out_shape=jax.ShapeDtypeStruct((t, d), F32),
        compiler_params=_cparams("parallel"),
        name="gmlp_final" if final else "gmlp",
    )(*args)


S5_OCT = LANES // S5_GROUP
S5_ROWS = 512


def _block_transpose8(v):
    blk = lax.broadcasted_iota(jnp.int32, v[0].shape, 1) // S5_GROUP
    for k in range(3):
        d = 1 << k
        low = ((blk >> k) & 1) == 0
        new = list(v)
        for i in range(S5_OCT):
            if (i >> k) & 1 == 0:
                a, b = v[i], v[i + d]
                new[i] = jnp.where(low, a, pltpu.roll(b, S5_GROUP * d, axis=1))
                new[i + d] = jnp.where(low, pltpu.roll(a, LANES - S5_GROUP * d, axis=1), b)
        v = new
    return v


def _s5_row_perm():
    n = 8 * S5_Q
    p = np.zeros((n, n), np.float32)
    for c in range(8):
        for s in range(S5_Q):
            p[s * 8 + c, c * S5_Q + s] = 1.0
    return p


def _s5_in_kernel(h_ref, ng_ref, win_ref, perm_ref, ur_ref, z_ref, u_sc, *, d_inner):
    hn = _rms(h_ref[...], ng_ref[...]).astype(BF16)
    z_ref[...] = _dot(hn, win_ref[:, d_inner:2 * d_inner]).astype(BF16)
    u = _dot(hn, win_ref[:, 0:d_inner]).astype(BF16)
    n_oct = d_inner // LANES
    q = S5_Q
    blk_rows = 8 * q
    halves = S5_ROWS // blk_rows
    for cb in range(halves):
        up = _dot(perm_ref[...], u[cb * blk_rows:(cb + 1) * blk_rows, :])
        for o in range(n_oct):
            u_sc[o, cb * blk_rows:(cb + 1) * blk_rows, :] = up[:, o * LANES:(o + 1) * LANES]

    def octet(o, carry):
        parts = []
        for cb in range(halves):
            row = []
            for sb in range(q // 8):
                rs = [u_sc[o, cb * blk_rows + (sb * 8 + si) * 8:cb * blk_rows + (sb * 8 + si + 1) * 8, :]
                      for si in range(8)]
                row.append(_block_transpose8(rs))
            parts.append(row)
        for gi in range(S5_OCT):
            blk = jnp.concatenate(
                [jnp.concatenate([parts[cb][sb][gi] for sb in range(q // 8)], axis=1) for cb in range(halves)], axis=0)
            ur_ref[o * S5_OCT + gi] = blk.astype(BF16)
        return carry

    lax.fori_loop(0, n_oct, octet, 0)


def _s5_in(h, norm_g, w_in):
    t, d = h.shape
    d_inner = w_in.shape[1] // 2
    g = d_inner // S5_GROUP
    tm = S5_ROWS
    return pl.pallas_call(
        functools.partial(_s5_in_kernel, d_inner=d_inner),
        grid=(t // tm,),
        in_specs=[pl.BlockSpec((tm, d), lambda i: (i, 0)), _const_spec((1, d)), _const_spec(w_in.shape),
                  _const_spec((8 * S5_Q, 8 * S5_Q))],
        out_specs=[pl.BlockSpec((g, tm // S5_Q, S5_Q * S5_GROUP), lambda i: (0, i, 0)),
                   pl.BlockSpec((tm, d_inner), lambda i: (i, 0))],
        out_shape=[jax.ShapeDtypeStruct((g, t // S5_Q, S5_Q * S5_GROUP), BF16),
                   jax.ShapeDtypeStruct((t, d_inner), BF16)],
        scratch_shapes=[pltpu.VMEM((d_inner // LANES, tm, LANES), F32)],
        compiler_params=_cparams("parallel"),
        name="s5_in",
    )(h, norm_g.reshape(1, d), w_in.astype(BF16), jnp.asarray(_s5_row_perm(), BF16))


S5_MAX_LEVELS = 8
S5_POW_ROWS = 40


def _s5_prep_kernel(are_ref, aim_ref, ls_ref, bx_ref, by_ref, cr_ref, ci_ref,
                    m_ref, p_ref, rt_ref, la_ref, lb_ref, a_sc):
    q, hh = S5_Q, S5_GROUP
    lane = lax.broadcasted_iota(jnp.int32, (1, LANES), 1)
    lo = lane < S5_STATE
    sgn = jnp.where(lo, -1.0, 1.0).astype(F32)

    step = jnp.exp(ls_ref[0])
    ar, ai = are_ref[0], aim_ref[0]
    lr, li = ar * step, ai * step
    j = lax.broadcasted_iota(jnp.int32, (S5_POW_ROWS, LANES), 0).astype(F32)
    mag = jnp.exp(j * lr)
    er, ei = mag * jnp.cos(j * li), mag * jnp.sin(j * li)
    nr, ni = er[1:2] - 1.0, ei[1:2]
    den = 1.0 / (ar * ar + ai * ai)
    cfr = (nr * ar + ni * ai) * den
    cfi = (ni * ar - nr * ai) * den

    bx, by = bx_ref[0], by_ref[0]
    bba = cfr * bx + cfi * sgn * by
    bbb = sgn * cfr * by - cfi * bx
    bbt = -(sgn * cfr * bx) - cfi * by

    c1 = cr_ref[0]
    c2 = sgn * ci_ref[0]
    ee1 = jnp.where(lo, er, ei)
    ee2 = jnp.where(lo, ei, er)
    for jj in range(q + 1):
        a_sc[jj * hh:(jj + 1) * hh, :] = c1 * ee1[jj:jj + 1] + c2 * ee2[jj:jj + 1]

    for s in range(q):
        e = q - 1 - s
        p_ref[0, s * hh:(s + 1) * hh, :] = (er[e:e + 1] * bba + ei[e:e + 1] * bbb).astype(BF16)

    rt_ref[0] = (-sgn * a_sc[hh:(q + 1) * hh, :]).astype(BF16)

    kt = _dot_nt(bbt, a_sc[0:q * hh, :], precision=lax.Precision.HIGHEST)
    col = lax.broadcasted_iota(jnp.int32, kt.shape, 1)
    for s in range(q):
        if s == 0:
            blk = kt
        else:
            blk = jnp.where(col >= s * hh, pltpu.roll(kt, s * hh, axis=1), 0.0)
        m_ref[0, s * hh:(s + 1) * hh, :] = blk.astype(BF16)

    pr, pi = er[q:q + 1], ei[q:q + 1]
    for k in range(S5_MAX_LEVELS):
        la_ref[0, k:k + 1, :] = pr
        lb_ref[0, k:k + 1, :] = sgn * pi
        pr, pi = pr * pr - pi * pi, 2.0 * pr * pi


def _s5_prep(a_re, a_im, log_step, b_re, b_im, c_re, c_im):
    g, p = a_re.shape
    hh = b_re.shape[2]
    qh = S5_Q * hh
    dup = lambda x: jnp.concatenate([x, x], axis=-1)
    are2 = dup(a_re).reshape(g, 1, 2 * p)
    aim2 = dup(a_im).reshape(g, 1, 2 * p)
    ls = log_step.reshape(g, 1, 1)
    brt, bit = jnp.swapaxes(b_re, 1, 2), jnp.swapaxes(b_im, 1, 2)
    bx = jnp.concatenate([brt, bit], axis=-1)
    by = jnp.concatenate([bit, brt], axis=-1)
    gspec = lambda *s: pl.BlockSpec((1,) + s, lambda i: (i,) + (0,) * len(s))
    return pl.pallas_call(
        _s5_prep_kernel,
        grid=(g,),
        in_specs=[gspec(1, 2 * p), gspec(1, 2 * p), gspec(1, 1), gspec(hh, 2 * p), gspec(hh, 2 * p),
                  gspec(hh, 2 * p), gspec(hh, 2 * p)],
        out_specs=[gspec(qh, qh), gspec(qh, 2 * p), gspec(qh, 2 * p), gspec(8, 2 * p), gspec(8, 2 * p)],
        out_shape=[jax.ShapeDtypeStruct((g, qh, qh), BF16), jax.ShapeDtypeStruct((g, qh, 2 * p), BF16),
                   jax.ShapeDtypeStruct((g, qh, 2 * p), BF16), jax.ShapeDtypeStruct((g, 8, 2 * p), F32),
                   jax.ShapeDtypeStruct((g, 8, 2 * p), F32)],
        scratch_shapes=[pltpu.VMEM(((S5_Q + 1) * hh, 2 * p), F32)],
        compiler_params=_cparams("parallel"),
        name="s5_prep",
    )(are2, aim2, ls, bx, by, dup(c_re), dup(c_im))


def _s5_core_kernel(u_ref, m_ref, p_ref, rt_ref, la_ref, lb_ref, d_ref, y_ref, yg_sc, *, nc, levels):
    n = u_ref.shape[1]
    q = S5_Q
    cidx = lax.broadcasted_iota(jnp.int32, (n, LANES), 0) & (nc - 1)
    for gi in range(S5_OCT):
        u = u_ref[gi]
        z = _dot(u, p_ref[gi])
        la, lb = la_ref[gi], lb_ref[gi]
        for k in range(levels):
            sh = 1 << k
            zs = jnp.where(cidx >= sh, pltpu.roll(z, sh, axis=0), 0.0)
            zw = pltpu.roll(zs, S5_STATE, axis=1)
            z = z + la[k:k + 1] * zs + lb[k:k + 1] * zw
        x = jnp.where(cidx >= 1, pltpu.roll(z, 1, axis=0), 0.0)
        y = _dot(u, m_ref[gi]) + _dot_nt(x.astype(BF16), rt_ref[gi]) + u.astype(F32) * d_ref[gi]
        yg_sc[gi] = _gelu(y)

    def rowgroup(rg, carry):
        r0 = pl.multiple_of(rg * 8, 8)
        out0 = pl.multiple_of(rg * 8 * q, 8 * q)
        for tb in range(q // 8):
            blocks = [yg_sc[gi, pl.ds(r0, 8), tb * LANES:(tb + 1) * LANES] for gi in range(S5_OCT)]
            rows = _block_transpose8(blocks)
            for ti in range(0, 8, 2):
                pair = jnp.concatenate([rows[ti], rows[ti + 1]], axis=0).astype(BF16)
                y_ref[pl.ds(out0 + (tb * 8 + ti) * 8, 16), :] = pair
        return carry

    lax.fori_loop(0, n // 8, rowgroup, 0, unroll=2)


def _s5_core(ur, m, p, rt, la, lb, dt, nc):
    g, n, qh = ur.shape
    levels = (nc - 1).bit_length()
    assert nc == 1 << levels and levels <= S5_MAX_LEVELS and n % nc == 0 and qh == S5_Q * S5_GROUP
    ospec = lambda *s: pl.BlockSpec((S5_OCT,) + s, lambda i: (i,) + (0,) * len(s))
    return pl.pallas_call(
        functools.partial(_s5_core_kernel, nc=nc, levels=levels),
        grid=(g // S5_OCT,),
        in_specs=[ospec(n, qh), ospec(qh, qh), ospec(qh, LANES), ospec(qh, LANES), ospec(8, LANES),
                  ospec(8, LANES), ospec(1, qh)],
        out_specs=pl.BlockSpec((n * S5_Q, LANES), lambda i: (0, i)),
        out_shape=jax.ShapeDtypeStruct((n * S5_Q, g * S5_GROUP), BF16),
        scratch_shapes=[pltpu.VMEM((S5_OCT, n, qh), F32)],
        compiler_params=_cparams("parallel"),
        name="s5_core",
    )(ur, m, p, rt, la, lb, dt)


def _s5_out_kernel(h_ref, y_ref, z_ref, permt_ref, wg_ref, bg_ref, wo_ref, o_ref):
    y = _dot(permt_ref[...], y_ref[...]).astype(BF16)
    g = _dot(y, wg_ref[...]) + bg_ref[...]
    y2 = y.astype(F32) * _sigmoid(g) * _silu(z_ref[...].astype(F32))
    o_ref[...] = h_ref[...] + _dot(y2.astype(BF16), wo_ref[...])


def _s5_out(h, y, z, w_glu, b_glu, w_out):
    t, d = h.shape
    d_inner = w_out.shape[0]
    tm = 8 * S5_Q
    return pl.pallas_call(
        _s5_out_kernel,
        grid=(t // tm,),
        in_specs=[pl.BlockSpec((tm, d), lambda i: (i, 0)), pl.BlockSpec((tm, d_inner), lambda i: (i, 0)),
                  pl.BlockSpec((tm, d_inner), lambda i: (i, 0)), _const_spec((tm, tm)), _const_spec(w_glu.shape),
                  _const_spec((1, d_inner)), _const_spec(w_out.shape)],
        out_specs=pl.BlockSpec((tm, d), lambda i: (i, 0)),
        out_shape=jax.ShapeDtypeStruct((t, d), F32),
        compiler_params=_cparams("parallel"),
        name="s5_out",
    )(h, y, z, jnp.asarray(_s5_row_perm().T, BF16), w_glu.astype(BF16), b_glu.reshape(1, d_inner),
      w_out.astype(BF16))


def _s5_layer(h, bsz, norm_g, w_in, a_re, a_im, log_step, b_re, b_im, c_re, c_im, d_skip, w_glu, b_glu, w_out):
    t, _ = h.shape
    seq = t // bsz
    g, _ = a_re.shape
    hh = b_re.shape[2]
    q = S5_Q
    nc = seq // q
    assert hh == S5_GROUP and seq % S5_ROWS == 0
    ur, z = _s5_in(h, norm_g, w_in)
    m, p, rt, la, lb = _s5_prep(a_re, a_im, log_step, b_re, b_im, c_re, c_im)
    dt = jnp.tile(d_skip.reshape(g, 1, hh), (1, 1, q))
    y = _s5_core(ur, m, p, rt, la, lb, dt, nc)
    return _s5_out(h, y, z, w_glu, b_glu, w_out)


def _rope_table_kernel(pos_ref, e_ref, cos_ref, sin_ref):
    p = pos_ref[...].astype(F32)
    ang = p[:, 0:1] * e_ref[0:1, :]
    for k in range(1, 4):
        ang = ang + p[:, k:k + 1] * e_ref[k:k + 1, :]
    cos_ref[...] = jnp.cos(ang)
    sin_ref[...] = jnp.sin(ang)


def _rope_tables(positions):
    t = positions.size
    half = MLA_ROPE // 2
    per_row = LANES // half
    inv_freq = (ROPE_THETA ** (-np.arange(0, MLA_ROPE, 2, dtype=np.float32) / np.float32(MLA_ROPE))).astype(np.float32)
    e = np.zeros((per_row, LANES), np.float32)
    for k in range(per_row):
        e[k, k * half:(k + 1) * half] = inv_freq
    rows = t // per_row
    tr = 512
    cos4, sin4 = pl.pallas_call(
        _rope_table_kernel,
        grid=(rows // tr,),
        in_specs=[pl.BlockSpec((tr, per_row), lambda i: (i, 0)), _const_spec((per_row, LANES))],
        out_specs=[pl.BlockSpec((tr, LANES), lambda i: (i, 0))] * 2,
        out_shape=[jax.ShapeDtypeStruct((rows, LANES), F32)] * 2,
        compiler_params=_cparams("parallel"),
        name="rope_table",
    )(positions.reshape(rows, per_row), jnp.asarray(e))
    return cos4.reshape(t, half), sin4.reshape(t, half)


def _rope128(x, c, s1, s2):
    return x * c - pltpu.roll(x, LANES - MLA_ROPE // 2, axis=1) * s1 + pltpu.roll(x, MLA_ROPE // 2, axis=1) * s2


def _mla_proj_kernel(h_ref, ng_ref, wa_ref, wz_ref, qg_ref, wuq_ref, kg_ref, wukv_ref, c_ref, s1_ref, s2_ref,
                     z_ref, q_ref, k_ref, v_ref, *, q_rank, kv_rank, scale):
    hn = _rms(h_ref[...], ng_ref[...]).astype(BF16)
    z_ref[...] = _dot(hn, wz_ref[...]).astype(BF16)
    a = _dot(hn, wa_ref[...])
    c, s1, s2 = c_ref[...], s1_ref[...], s2_ref[...]
    cq = _rms(a[:, 0:q_rank], qg_ref[...]).astype(BF16)
    ckv = _rms(a[:, q_rank:q_rank + kv_rank], kg_ref[...]).astype(BF16)
    kr = _rope128(a[:, q_rank + kv_rank:q_rank + kv_rank + LANES], c, s1, s2).astype(BF16)
    q = _dot(cq, wuq_ref[...]) * scale
    kv = _dot(ckv, wukv_ref[...])
    for hd in range(MLA_HEADS):
        o = hd * MLA_QK_PAD
        q_ref[:, o:o + MLA_NOPE] = q[:, o:o + MLA_NOPE].astype(BF16)
        q_ref[:, o + MLA_NOPE:o + MLA_QK_PAD] = _rope128(q[:, o + MLA_NOPE:o + MLA_QK_PAD], c, s1, s2).astype(BF16)
        k_ref[:, o:o + MLA_NOPE] = kv[:, o:o + MLA_NOPE].astype(BF16)
        k_ref[:, o + MLA_NOPE:o + MLA_QK_PAD] = kr
        v_ref[:, hd * MLA_V:(hd + 1) * MLA_V] = kv[:, o + MLA_NOPE:o + MLA_NOPE + MLA_V].astype(BF16)


def _mla_proj(h, norm_g, w_in, q_norm_g, w_uq, kv_norm_g, w_ukv, cos, sin):
    t, d = h.shape
    q_rank, kv_rank = w_uq.shape[0], w_ukv.shape[0]
    n_a = q_rank + kv_rank + MLA_ROPE
    d_inner = w_in.shape[1] - n_a
    tm = ROW_BLOCK
    wa = jnp.pad(w_in[:, :n_a], ((0, 0), (0, LANES - MLA_ROPE))).astype(BF16)
    wz = w_in[:, n_a:].astype(BF16)
    qk = MLA_NOPE + MLA_ROPE
    wuq = jnp.pad(w_uq.reshape(q_rank, MLA_HEADS, qk), ((0, 0), (0, 0), (0, MLA_QK_PAD - qk)))
    wuq = wuq.reshape(q_rank, MLA_HEADS * MLA_QK_PAD).astype(BF16)
    half = MLA_ROPE // 2
    zero = jnp.zeros_like(cos)
    ctab = jnp.concatenate([cos, cos, zero, zero], axis=1)
    s1tab = jnp.concatenate([sin, zero, zero, zero], axis=1)
    s2tab = jnp.concatenate([zero, sin, zero, zero], axis=1)
    assert ctab.shape[1] == LANES and half * 4 == LANES
    row = lambda n: pl.BlockSpec((tm, n), lambda i: (i, 0))
    return pl.pallas_call(
        functools.partial(_mla_proj_kernel, q_rank=q_rank, kv_rank=kv_rank, scale=float(qk) ** -0.5 * math.log2(math.e)),
        grid=(t // tm,),
        in_specs=[row(d), _const_spec((1, d)), _const_spec(wa.shape), _const_spec(wz.shape),
                  _const_spec((1, q_rank)), _const_spec(wuq.shape), _const_spec((1, kv_rank)),
                  _const_spec(w_ukv.shape), row(LANES), row(LANES), row(LANES)],
        out_specs=[row(d_inner), row(MLA_HEADS * MLA_QK_PAD), row(MLA_HEADS * MLA_QK_PAD), row(MLA_HEADS * MLA_V)],
        out_shape=[jax.ShapeDtypeStruct((t, d_inner), BF16),
                   jax.ShapeDtypeStruct((t, MLA_HEADS * MLA_QK_PAD), BF16),
                   jax.ShapeDtypeStruct((t, MLA_HEADS * MLA_QK_PAD), BF16),
                   jax.ShapeDtypeStruct((t, MLA_HEADS * MLA_V), BF16)],
        compiler_params=_cparams("parallel"),
        name="mla_proj",
    )(h, norm_g.reshape(1, d), wa, wz, q_norm_g.reshape(1, q_rank), wuq, kv_norm_g.reshape(1, kv_rank),
      w_ukv.astype(BF16), ctab, s1tab, s2tab)


ATTN_TQ = 1024
ATTN_TK = 512


def _attn_kernel_pipe(q_ref, k_ref, v_ref, o_ref, vx_sc, acc_sc, m_sc, p0_sc, p1_sc, al0_sc, al1_sc):
    qi = pl.program_id(2)
    tq, tk = ATTN_TQ, ATTN_TK
    p_sc, al_sc = (p0_sc, p1_sc), (al0_sc, al1_sc)

    @pl.when(qi == 0)
    def _():
        vx_sc[:, 0:MLA_V] = v_ref[...]
        lane = lax.broadcasted_iota(jnp.int32, (vx_sc.shape[0], LANES), 1)
        vx_sc[:, MLA_V:MLA_V + LANES] = jnp.where(lane == 0, 1.0, 0.0).astype(BF16)

    m_sc[...] = jnp.full_like(m_sc, NEG_INF)
    acc_sc[...] = jnp.zeros_like(acc_sc)

    def scores(koff, slot, mask_off=None):
        s = _dot_nt(q_ref[...], k_ref[pl.ds(koff, tk), :])
        if mask_off is not None:
            r = lax.broadcasted_iota(jnp.int32, s.shape, 0)
            c = lax.broadcasted_iota(jnp.int32, s.shape, 1) + mask_off
            s = jnp.where(c <= r, s, NEG_INF)
        m_prev = m_sc[...]
        m_new = jnp.maximum(m_prev, jnp.max(s, axis=-1, keepdims=True))
        al_sc[slot][...] = jnp.exp2(m_prev - m_new)
        p_sc[slot][...] = jnp.exp2(s - m_new).astype(BF16)
        m_sc[...] = m_new

    def accumulate(koff, slot):
        pv = _dot(p_sc[slot][...], vx_sc[pl.ds(koff, tk), :])
        acc_sc[...] = al_sc[slot][...] * acc_sc[...] + pv

    diag = pl.multiple_of(qi * tq, tq)
    scores(diag, 0, mask_off=0)
    accumulate(diag, 0)
    scores(diag + tk, 1, mask_off=tk)

    def body(i, carry):
        k0 = pl.multiple_of(2 * i * tk, tk)
        prev = jnp.where(i == 0, diag + tk, k0 - tk)
        accumulate(pl.multiple_of(prev, tk), 1)
        scores(k0, 0)
        accumulate(k0, 0)
        scores(k0 + tk, 1)
        return carry

    lax.fori_loop(0, qi, body, 0)
    last = jnp.where(qi == 0, diag + tk, diag - tk)
    accumulate(pl.multiple_of(last, tk), 1)
    acc = acc_sc[...]
    o_ref[...] = (acc[:, 0:MLA_V] / acc[:, MLA_V:MLA_V + 1]).astype(o_ref.dtype)


def _attention_pipe(q, k, v, bsz):
    t = q.shape[0]
    seq = t // bsz
    assert ATTN_TQ == 2 * ATTN_TK and seq % ATTN_TQ == 0
    nq = seq // ATTN_TQ
    qmap = lambda b, h, i: (b * nq + i, h)
    kmap = lambda b, h, i: (b, h)
    return pl.pallas_call(
        _attn_kernel,
        grid=(bsz, MLA_HEADS, nq),
        in_specs=[pl.BlockSpec((ATTN_TQ, MLA_QK_PAD), qmap), pl.BlockSpec((seq, MLA_QK_PAD), kmap),
                  pl.BlockSpec((seq, MLA_V), kmap)],
        out_specs=pl.BlockSpec((ATTN_TQ, MLA_V), qmap),
        out_shape=jax.ShapeDtypeStruct((t, MLA_HEADS * MLA_V), BF16),
        scratch_shapes=[pltpu.VMEM((seq, MLA_V + LANES), BF16), pltpu.VMEM((ATTN_TQ, MLA_V + LANES), F32),
                        pltpu.VMEM((ATTN_TQ, 1), F32),
                        pltpu.VMEM((ATTN_TQ, ATTN_TK), BF16), pltpu.VMEM((ATTN_TQ, ATTN_TK), BF16),
                        pltpu.VMEM((ATTN_TQ, 1), F32), pltpu.VMEM((ATTN_TQ, 1), F32)],
        compiler_params=_cparams("parallel", "parallel", "arbitrary"),
        name="mla_attn",
    )(q, k, v)


ATT_K = 512
ATT_CHAINS = 2
ATT_Q = ATT_CHAINS * ATT_K
ATT_VROWS = MLA_V + 16


def _attn_kernel(q_ref, k_ref, v_ref, o_ref, vxt_sc, acc_sc, m_sc, st0_sc, st1_sc, pt0_sc, pt1_sc, al0_sc, al1_sc):
    qi = pl.program_id(2)
    n_kt = vxt_sc.shape[0]

    @pl.when(qi == 0)
    def _():
        row = lax.broadcasted_iota(jnp.int32, (ATT_VROWS - MLA_V, ATT_K), 0)
        ones = jnp.where(row == 0, 1.0, 0.0).astype(BF16)
        for j in range(n_kt):
            vt = v_ref[j * ATT_K:(j + 1) * ATT_K, :].astype(F32).T
            vxt_sc[j, 0:MLA_V, :] = vt.astype(BF16)
            vxt_sc[j, MLA_V:ATT_VROWS, :] = ones

    m_sc[...] = jnp.full_like(m_sc, NEG_INF)
    acc_sc[...] = jnp.zeros_like(acc_sc)

    st_sc, pt_sc, al_sc = (st0_sc, st1_sc), (pt0_sc, pt1_sc), (al0_sc, al1_sc)

    def qk(c, j):
        koff = pl.multiple_of(j * ATT_K, ATT_K)
        st_sc[c][...] = _dot_nt(k_ref[pl.ds(koff, ATT_K), :], q_ref[c * ATT_K:(c + 1) * ATT_K, :])

    def softmax(c, masked):
        qs = slice(c * ATT_K, (c + 1) * ATT_K)
        st = st_sc[c][...]
        if masked:
            kk = lax.broadcasted_iota(jnp.int32, st.shape, 0)
            qq = lax.broadcasted_iota(jnp.int32, st.shape, 1)
            st = jnp.where(kk <= qq, st, NEG_INF)
        m_prev = m_sc[:, qs]
        m_new = jnp.maximum(m_prev, jnp.max(st, axis=0, keepdims=True))
        al_sc[c][...] = jnp.exp2(m_prev - m_new)
        pt_sc[c][...] = jnp.exp2(st - m_new).astype(BF16)
        m_sc[:, qs] = m_new

    def pv(c, j):
        qs = slice(c * ATT_K, (c + 1) * ATT_K)
        acc_sc[:, qs] = al_sc[c][...] * acc_sc[:, qs] + _dot(vxt_sc[j], pt_sc[c][...])

    base = qi * ATT_CHAINS
    qk(0, 0)

    def one_tile(j):
        qk(1, j)
        softmax(0, False)
        pv(0, j)
        softmax(1, False)
        qk(0, j + 1)
        pv(1, j)

    def body(i, carry):
        one_tile(2 * i)
        one_tile(2 * i + 1)
        return carry

    lax.fori_loop(0, qi, body, 0)
    qk(1, base)
    softmax(0, True)
    pv(0, base)
    softmax(1, False)
    qk(1, base + 1)
    pv(1, base)
    softmax(1, True)
    pv(1, base + 1)
    acc = acc_sc[...]
    o = acc[0:MLA_V, :] / acc[MLA_V:MLA_V + 1, :]
    o_ref[...] = o.T.astype(o_ref.dtype)


def _attention(q, k, v, bsz):
    t = q.shape[0]
    seq = t // bsz
    assert seq % ATT_Q == 0
    nq = seq // ATT_Q
    qmap = lambda b, h, i: (b * nq + i, h)
    kmap = lambda b, h, i: (b, h)
    return pl.pallas_call(
        _attn_kernel,
        grid=(bsz, MLA_HEADS, nq),
        in_specs=[pl.BlockSpec((ATT_Q, MLA_QK_PAD), qmap), pl.BlockSpec((seq, MLA_QK_PAD), kmap),
                  pl.BlockSpec((seq, MLA_V), kmap)],
        out_specs=pl.BlockSpec((ATT_Q, MLA_V), qmap),
        out_shape=jax.ShapeDtypeStruct((t, MLA_HEADS * MLA_V), BF16),
        scratch_shapes=[pltpu.VMEM((seq // ATT_K, ATT_VROWS, ATT_K), BF16), pltpu.VMEM((ATT_VROWS, ATT_Q), F32),
                        pltpu.VMEM((1, ATT_Q), F32),
                        pltpu.VMEM((ATT_K, ATT_K), F32), pltpu.VMEM((ATT_K, ATT_K), F32),
                        pltpu.VMEM((ATT_K, ATT_K), BF16), pltpu.VMEM((ATT_K, ATT_K), BF16),
                        pltpu.VMEM((1, ATT_K), F32), pltpu.VMEM((1, ATT_K), F32)],
        compiler_params=_cparams("parallel", "parallel", "arbitrary"),
        name="mla_attn",
    )(q, k, v)


def _mla_out_kernel(h_ref, o_ref, z_ref, wo_ref, out_ref):
    y = o_ref[...].astype(F32) * _silu(z_ref[...].astype(F32))
    out_ref[...] = h_ref[...] + _dot(y.astype(BF16), wo_ref[...])


def _mla_out(h, o, z, w_out):
    t, d = h.shape
    d_inner = w_out.shape[0]
    tm = ROW_BLOCK
    return pl.pallas_call(
        _mla_out_kernel,
        grid=(t // tm,),
        in_specs=[pl.BlockSpec((tm, d), lambda i: (i, 0)), pl.BlockSpec((tm, d_inner), lambda i: (i, 0)),
                  pl.BlockSpec((tm, d_inner), lambda i: (i, 0)), _const_spec(w_out.shape)],
        out_specs=pl.BlockSpec((tm, d), lambda i: (i, 0)),
        out_shape=jax.ShapeDtypeStruct((t, d), F32),
        compiler_params=_cparams("parallel"),
        name="mla_out",
    )(h, o, z, w_out.astype(BF16))


def _mla_layer(h, bsz, positions, norm_g, w_in, q_norm_g, w_uq, kv_norm_g, w_ukv, w_out):
    cos, sin = _rope_tables(positions)
    z, q, k, v = _mla_proj(h, norm_g, w_in, q_norm_g, w_uq, kv_norm_g, w_ukv, cos, sin)
    o = _attention(q, k, v, bsz)
    return _mla_out(h, o, z, w_out)


def kernel(x, positions, l0_norm_g, l0_w_in, l0_ln_g, l0_ln_b, l0_w_s, l0_b_s, l0_w_out, l1_norm_g, l1_w_in, l1_a_re, l1_a_im, l1_log_step, l1_b_re, l1_b_im, l1_c_re, l1_c_im, l1_d_skip, l1_w_glu, l1_b_glu, l1_w_out, l2_norm_g, l2_w_in, l2_q_norm_g, l2_w_uq, l2_kv_norm_g, l2_w_ukv, l2_w_out, l3_norm_g, l3_w_in, l3_ln_g, l3_ln_b, l3_w_s, l3_b_s, l3_w_out, final_norm_g):
    bsz, seq, d = x.shape
    h = x.reshape(bsz * seq, d)
    h = _gmlp_layer(h, l0_norm_g, l0_w_in, l0_ln_g, l0_ln_b, l0_w_s, l0_b_s, l0_w_out)
    h = _s5_layer(h, bsz, l1_norm_g, l1_w_in, l1_a_re, l1_a_im, l1_log_step, l1_b_re, l1_b_im, l1_c_re, l1_c_im,
                  l1_d_skip, l1_w_glu, l1_b_glu, l1_w_out)
    h = _mla_layer(h, bsz, positions, l2_norm_g, l2_w_in, l2_q_norm_g, l2_w_uq, l2_kv_norm_g, l2_w_ukv, l2_w_out)
    h = _gmlp_layer(h, l3_norm_g, l3_w_in, l3_ln_g, l3_ln_b, l3_w_s, l3_b_s, l3_w_out, final_g=final_norm_g)
    return h.reshape(bsz, seq, d)
```

```python
import functools
import math

import jax
import jax.numpy as jnp
import numpy as np
from jax import lax
from jax.experimental import pallas as pl
from jax.experimental.pallas import tpu as pltpu

F32 = jnp.float32
BF16 = jnp.bfloat16

NORM_EPS = 1e-6
LANES = 128
VMEM_LIMIT = 56 * 1024 * 1024

GMLP_CHUNK = 128
GMLP_GROUPS = 8
S5_GROUP = 16
S5_STATE = 64
S5_Q = 32
MLA_HEADS = 16
MLA_NOPE = 128
MLA_ROPE = 64
MLA_V = 128
MLA_QK_PAD = 256
ROPE_THETA = 10000.0
NEG_INF = -1e30

ROW_BLOCK = 512


def _cparams(*sem):
    return pltpu.CompilerParams(dimension_semantics=sem, vmem_limit_bytes=VMEM_LIMIT)


def _const_spec(shape):
    nd = len(shape)
    return pl.BlockSpec(shape, lambda *_: (0,) * nd, pipeline_mode=pl.Buffered(1))


def _dot(a, b):
    return jnp.dot(a, b, preferred_element_type=F32)


def _dot_nt(a, b, precision=None):
    return lax.dot_general(a, b, (((1,), (1,)), ((), ())), preferred_element_type=F32, precision=precision)


def _rms(x, g):
    return x * lax.rsqrt(jnp.mean(x * x, axis=-1, keepdims=True) + NORM_EPS) * g


def _gelu(x):
    return 0.5 * x * (1.0 + jnp.tanh(math.sqrt(2.0 / math.pi) * (x + 0.044715 * (x * x * x))))


def _sigmoid(x):
    return 0.5 * (1.0 + jnp.tanh(0.5 * x))


def _silu(x):
    return x * _sigmoid(x)


def _gmlp_kernel(h_ref, ng_ref, win_ref, lng_ref, lnb_ref, ws_ref, bst_ref, wout_ref, *rest, d_inner, final):
    if final:
        fg_ref, o_ref = rest
    else:
        (o_ref,) = rest
    x = h_ref[...]
    tm = x.shape[0]
    hn = _rms(x, ng_ref[...]).astype(BF16)
    u = _gelu(_dot(hn, win_ref[:, 0:d_inner]))
    v = _gelu(_dot(hn, win_ref[:, d_inner:2 * d_inner]))
    mu = jnp.mean(v, axis=-1, keepdims=True)
    vc = v - mu
    var = jnp.mean(vc * vc, axis=-1, keepdims=True)
    vn = (vc * lax.rsqrt(var + NORM_EPS) * lng_ref[...] + lnb_ref[...]).astype(BF16)
    z = _dot(hn, win_ref[:, 2 * d_inner:3 * d_inner])
    gate = u * _silu(z)

    gw = d_inner // GMLP_GROUPS
    r = lax.broadcasted_iota(jnp.int32, (GMLP_CHUNK, GMLP_CHUNK), 0)
    c = lax.broadcasted_iota(jnp.int32, (GMLP_CHUNK, GMLP_CHUNK), 1)
    causal = c <= r
    cols = []
    for g in range(GMLP_GROUPS):
        w = jnp.where(causal, ws_ref[g], 0.0).astype(BF16)
        bias = bst_ref[:, g:g + 1]
        rows = []
        for ci in range(tm // GMLP_CHUNK):
            vs = vn[ci * GMLP_CHUNK:(ci + 1) * GMLP_CHUNK, g * gw:(g + 1) * gw]
            rows.append(_dot(w, vs) + bias)
        cols.append(jnp.concatenate(rows, axis=0) if len(rows) > 1 else rows[0])
    s = jnp.concatenate(cols, axis=1)
    y = (gate * s).astype(BF16)
    out = x + _dot(y, wout_ref[...])
    if final:
        out = _rms(out, fg_ref[...])
    o_ref[...] = out


def _gmlp_layer(h, norm_g, w_in, ln_g, ln_b, w_s, b_s, w_out, final_g=None):
    t, d = h.shape
    d_inner = w_out.shape[0]
    tm = ROW_BLOCK
    final = final_g is not None
    args = [h, norm_g.reshape(1, d), w_in.astype(BF16), ln_g.reshape(1, d_inner), ln_b.reshape(1, d_inner),
            w_s, b_s.T, w_out.astype(BF16)]
    specs = [pl.BlockSpec((tm, d), lambda i: (i, 0)), _const_spec((1, d)), _const_spec(w_in.shape),
             _const_spec((1, d_inner)), _const_spec((1, d_inner)), _const_spec(w_s.shape),
             _const_spec((GMLP_CHUNK, GMLP_GROUPS)), _const_spec(w_out.shape)]
    if final:
        args.append(final_g.reshape(1, d))
        specs.append(_const_spec((1, d)))
    return pl.pallas_call(
        functools.partial(_gmlp_kernel, d_inner=d_inner, final=final),
        grid=(t // tm,),
        in_specs=specs,
        out_specs=pl.BlockSpec((tm, d), lambda i: (i, 0)),
        out_shape=jax.ShapeDtypeStruct((t, d), F32),
        compiler_params=_cparams("parallel"),
        name="gmlp_final" if final else "gmlp",
    )(*args)


S5_OCT = LANES // S5_GROUP
S5_ROWS = 512


def _block_transpose8(v):
    blk = lax.broadcasted_iota(jnp.int32, v[0].shape, 1) // S5_GROUP
    for k in range(3):
        d = 1 << k
        low = ((blk >> k) & 1) == 0
        new = list(v)
        for i in range(S5_OCT):
            if (i >> k) & 1 == 0:
                a, b = v[i], v[i + d]
                new[i] = jnp.where(low, a, pltpu.roll(b, S5_GROUP * d, axis=1))
                new[i + d] = jnp.where(low, pltpu.roll(a, LANES - S5_GROUP * d, axis=1), b)
        v = new
    return v


def _s5_row_perm():
    n = 8 * S5_Q
    p = np.zeros((n, n), np.float32)
    for c in range(8):
        for s in range(S5_Q):
            p[s * 8 + c, c * S5_Q + s] = 1.0
    return p


def _s5_in_kernel(h_ref, ng_ref, win_ref, perm_ref, ur_ref, z_ref, u_sc, *, d_inner):
    hn = _rms(h_ref[...], ng_ref[...]).astype(BF16)
    z_ref[...] = _dot(hn, win_ref[:, d_inner:2 * d_inner]).astype(BF16)
    u = _dot(hn, win_ref[:, 0:d_inner]).astype(BF16)
    n_oct = d_inner // LANES
    q = S5_Q
    blk_rows = 8 * q
    halves = S5_ROWS // blk_rows
    for cb in range(halves):
        up = _dot(perm_ref[...], u[cb * blk_rows:(cb + 1) * blk_rows, :])
        for o in range(n_oct):
            u_sc[o, cb * blk_rows:(cb + 1) * blk_rows, :] = up[:, o * LANES:(o + 1) * LANES]

    def octet(o, carry):
        parts = []
        for cb in range(halves):
            row = []
            for sb in range(q // 8):
                rs = [u_sc[o, cb * blk_rows + (sb * 8 + si) * 8:cb * blk_rows + (sb * 8 + si + 1) * 8, :]
                      for si in range(8)]
                row.append(_block_transpose8(rs))
            parts.append(row)
        for gi in range(S5_OCT):
            blk = jnp.concatenate(
                [jnp.concatenate([parts[cb][sb][gi] for sb in range(q // 8)], axis=1) for cb in range(halves)], axis=0)
            ur_ref[o * S5_OCT + gi] = blk.astype(BF16)
        return carry

    lax.fori_loop(0, n_oct, octet, 0)


def _s5_in(h, norm_g, w_in):
    t, d = h.shape
    d_inner = w_in.shape[1] // 2
    g = d_inner // S5_GROUP
    tm = S5_ROWS
    return pl.pallas_call(
        functools.partial(_s5_in_kernel, d_inner=d_inner),
        grid=(t // tm,),
        in_specs=[pl.BlockSpec((tm, d), lambda i: (i, 0)), _const_spec((1, d)), _const_spec(w_in.shape),
                  _const_spec((8 * S5_Q, 8 * S5_Q))],
        out_specs=[pl.BlockSpec((g, tm // S5_Q, S5_Q * S5_GROUP), lambda i: (0, i, 0)),
                   pl.BlockSpec((tm, d_inner), lambda i: (i, 0))],
        out_shape=[jax.ShapeDtypeStruct((g, t // S5_Q, S5_Q * S5_GROUP), BF16),
                   jax.ShapeDtypeStruct((t, d_inner), BF16)],
        scratch_shapes=[pltpu.VMEM((d_inner // LANES, tm, LANES), F32)],
        compiler_params=_cparams("parallel"),
        name="s5_in",
    )(h, norm_g.reshape(1, d), w_in.astype(BF16), jnp.asarray(_s5_row_perm(), BF16))


S5_MAX_LEVELS = 8
S5_POW_ROWS = 40


def _s5_prep_kernel(are_ref, aim_ref, ls_ref, bx_ref, by_ref, cr_ref, ci_ref,
                    m_ref, p_ref, rt_ref, la_ref, lb_ref, a_sc):
    q, hh = S5_Q, S5_GROUP
    lane = lax.broadcasted_iota(jnp.int32, (1, LANES), 1)
    lo = lane < S5_STATE
    sgn = jnp.where(lo, -1.0, 1.0).astype(F32)

    step = jnp.exp(ls_ref[0])
    ar, ai = are_ref[0], aim_ref[0]
    lr, li = ar * step, ai * step
    j = lax.broadcasted_iota(jnp.int32, (S5_POW_ROWS, LANES), 0).astype(F32)
    mag = jnp.exp(j * lr)
    er, ei = mag * jnp.cos(j * li), mag * jnp.sin(j * li)
    nr, ni = er[1:2] - 1.0, ei[1:2]
    den = 1.0 / (ar * ar + ai * ai)
    cfr = (nr * ar + ni * ai) * den
    cfi = (ni * ar - nr * ai) * den

    bx, by = bx_ref[0], by_ref[0]
    bba = cfr * bx + cfi * sgn * by
    bbb = sgn * cfr * by - cfi * bx
    bbt = -(sgn * cfr * bx) - cfi * by

    c1 = cr_ref[0]
    c2 = sgn * ci_ref[0]
    ee1 = jnp.where(lo, er, ei)
    ee2 = jnp.where(lo, ei, er)
    for jj in range(q + 1):
        a_sc[jj * hh:(jj + 1) * hh, :] = c1 * ee1[jj:jj + 1] + c2 * ee2[jj:jj + 1]

    for s in range(q):
        e = q - 1 - s
        p_ref[0, s * hh:(s + 1) * hh, :] = (er[e:e + 1] * bba + ei[e:e + 1] * bbb).astype(BF16)

    rt_ref[0] = (-sgn * a_sc[hh:(q + 1) * hh, :]).astype(BF16)

    kt = _dot_nt(bbt, a_sc[0:q * hh, :], precision=lax.Precision.HIGHEST)
    col = lax.broadcasted_iota(jnp.int32, kt.shape, 1)
    for s in range(q):
        if s == 0:
            blk = kt
        else:
            blk = jnp.where(col >= s * hh, pltpu.roll(kt, s * hh, axis=1), 0.0)
        m_ref[0, s * hh:(s + 1) * hh, :] = blk.astype(BF16)

    pr, pi = er[q:q + 1], ei[q:q + 1]
    for k in range(S5_MAX_LEVELS):
        la_ref[0, k:k + 1, :] = pr
        lb_ref[0, k:k + 1, :] = sgn * pi
        pr, pi = pr * pr - pi * pi, 2.0 * pr * pi


def _s5_prep(a_re, a_im, log_step, b_re, b_im, c_re, c_im):
    g, p = a_re.shape
    hh = b_re.shape[2]
    qh = S5_Q * hh
    dup = lambda x: jnp.concatenate([x, x], axis=-1)
    are2 = dup(a_re).reshape(g, 1, 2 * p)
    aim2 = dup(a_im).reshape(g, 1, 2 * p)
    ls = log_step.reshape(g, 1, 1)
    brt, bit = jnp.swapaxes(b_re, 1, 2), jnp.swapaxes(b_im, 1, 2)
    bx = jnp.concatenate([brt, bit], axis=-1)
    by = jnp.concatenate([bit, brt], axis=-1)
    gspec = lambda *s: pl.BlockSpec((1,) + s, lambda i: (i,) + (0,) * len(s))
    return pl.pallas_call(
        _s5_prep_kernel,
        grid=(g,),
        in_specs=[gspec(1, 2 * p), gspec(1, 2 * p), gspec(1, 1), gspec(hh, 2 * p), gspec(hh, 2 * p),
                  gspec(hh, 2 * p), gspec(hh, 2 * p)],
        out_specs=[gspec(qh, qh), gspec(qh, 2 * p), gspec(qh, 2 * p), gspec(8, 2 * p), gspec(8, 2 * p)],
        out_shape=[jax.ShapeDtypeStruct((g, qh, qh), BF16), jax.ShapeDtypeStruct((g, qh, 2 * p), BF16),
                   jax.ShapeDtypeStruct((g, qh, 2 * p), BF16), jax.ShapeDtypeStruct((g, 8, 2 * p), F32),
                   jax.ShapeDtypeStruct((g, 8, 2 * p), F32)],
        scratch_shapes=[pltpu.VMEM(((S5_Q + 1) * hh, 2 * p), F32)],
        compiler_params=_cparams("parallel"),
        name="s5_prep",
    )(are2, aim2, ls, bx, by, dup(c_re), dup(c_im))


def _s5_core_kernel(u_ref, m_ref, p_ref, rt_ref, la_ref, lb_ref, d_ref, y_ref, yg_sc, *, nc, levels):
    n = u_ref.shape[1]
    q = S5_Q
    cidx = lax.broadcasted_iota(jnp.int32, (n, LANES), 0) & (nc - 1)
    for gi in range(S5_OCT):
        u = u_ref[gi]
        z = _dot(u, p_ref[gi])
        la, lb = la_ref[gi], lb_ref[gi]
        for k in range(levels):
            sh = 1 << k
            zs = jnp.where(cidx >= sh, pltpu.roll(z, sh, axis=0), 0.0)
            zw = pltpu.roll(zs, S5_STATE, axis=1)
            z = z + la[k:k + 1] * zs + lb[k:k + 1] * zw
        x = jnp.where(cidx >= 1, pltpu.roll(z, 1, axis=0), 0.0)
        y = _dot(u, m_ref[gi]) + _dot_nt(x.astype(BF16), rt_ref[gi]) + u.astype(F32) * d_ref[gi]
        yg_sc[gi] = _gelu(y)

    def rowgroup(rg, carry):
        r0 = pl.multiple_of(rg * 8, 8)
        out0 = pl.multiple_of(rg * 8 * q, 8 * q)
        for tb in range(q // 8):
            blocks = [yg_sc[gi, pl.ds(r0, 8), tb * LANES:(tb + 1) * LANES] for gi in range(S5_OCT)]
            rows = _block_transpose8(blocks)
            for ti in range(0, 8, 2):
                pair = jnp.concatenate([rows[ti], rows[ti + 1]], axis=0).astype(BF16)
                y_ref[pl.ds(out0 + (tb * 8 + ti) * 8, 16), :] = pair
        return carry

    lax.fori_loop(0, n // 8, rowgroup, 0, unroll=2)


def _s5_core(ur, m, p, rt, la, lb, dt, nc):
    g, n, qh = ur.shape
    levels = (nc - 1).bit_length()
    assert nc == 1 << levels and levels <= S5_MAX_LEVELS and n % nc == 0 and qh == S5_Q * S5_GROUP
    ospec = lambda *s: pl.BlockSpec((S5_OCT,) + s, lambda i: (i,) + (0,) * len(s))
    return pl.pallas_call(
        functools.partial(_s5_core_kernel, nc=nc, levels=levels),
        grid=(g // S5_OCT,),
        in_specs=[ospec(n, qh), ospec(qh, qh), ospec(qh, LANES), ospec(qh, LANES), ospec(8, LANES),
                  ospec(8, LANES), ospec(1, qh)],
        out_specs=pl.BlockSpec((n * S5_Q, LANES), lambda i: (0, i)),
        out_shape=jax.ShapeDtypeStruct((n * S5_Q, g * S5_GROUP), BF16),
        scratch_shapes=[pltpu.VMEM((S5_OCT, n, qh), F32)],
        compiler_params=_cparams("parallel"),
        name="s5_core",
    )(ur, m, p, rt, la, lb, dt)


def _s5_out_kernel(h_ref, y_ref, z_ref, permt_ref, wg_ref, bg_ref, wo_ref, o_ref):
    y = _dot(permt_ref[...], y_ref[...]).astype(BF16)
    g = _dot(y, wg_ref[...]) + bg_ref[...]
    y2 = y.astype(F32) * _sigmoid(g) * _silu(z_ref[...].astype(F32))
    o_ref[...] = h_ref[...] + _dot(y2.astype(BF16), wo_ref[...])


def _s5_out(h, y, z, w_glu, b_glu, w_out):
    t, d = h.shape
    d_inner = w_out.shape[0]
    tm = 8 * S5_Q
    return pl.pallas_call(
        _s5_out_kernel,
        grid=(t // tm,),
        in_specs=[pl.BlockSpec((tm, d), lambda i: (i, 0)), pl.BlockSpec((tm, d_inner), lambda i: (i, 0)),
                  pl.BlockSpec((tm, d_inner), lambda i: (i, 0)), _const_spec((tm, tm)), _const_spec(w_glu.shape),
                  _const_spec((1, d_inner)), _const_spec(w_out.shape)],
        out_specs=pl.BlockSpec((tm, d), lambda i: (i, 0)),
        out_shape=jax.ShapeDtypeStruct((t, d), F32),
        compiler_params=_cparams("parallel"),
        name="s5_out",
    )(h, y, z, jnp.asarray(_s5_row_perm().T, BF16), w_glu.astype(BF16), b_glu.reshape(1, d_inner),
      w_out.astype(BF16))


def _s5_layer(h, bsz, norm_g, w_in, a_re, a_im, log_step, b_re, b_im, c_re, c_im, d_skip, w_glu, b_glu, w_out):
    t, _ = h.shape
    seq = t // bsz
    g, _ = a_re.shape
    hh = b_re.shape[2]
    q = S5_Q
    nc = seq // q
    assert hh == S5_GROUP and seq % S5_ROWS == 0
    ur, z = _s5_in(h, norm_g, w_in)
    m, p, rt, la, lb = _s5_prep(a_re, a_im, log_step, b_re, b_im, c_re, c_im)
    dt = jnp.tile(d_skip.reshape(g, 1, hh), (1, 1, q))
    y = _s5_core(ur, m, p, rt, la, lb, dt, nc)
    return _s5_out(h, y, z, w_glu, b_glu, w_out)


def _rope_table_kernel(pos_ref, e_ref, cos_ref, sin_ref):
    p = pos_ref[...].astype(F32)
    ang = p[:, 0:1] * e_ref[0:1, :]
    for k in range(1, 4):
        ang = ang + p[:, k:k + 1] * e_ref[k:k + 1, :]
    cos_ref[...] = jnp.cos(ang)
    sin_ref[...] = jnp.sin(ang)


def _rope_tables(positions):
    t = positions.size
    half = MLA_ROPE // 2
    per_row = LANES // half
    inv_freq = (ROPE_THETA ** (-np.arange(0, MLA_ROPE, 2, dtype=np.float32) / np.float32(MLA_ROPE))).astype(np.float32)
    e = np.zeros((per_row, LANES), np.float32)
    for k in range(per_row):
        e[k, k * half:(k + 1) * half] = inv_freq
    rows = t // per_row
    tr = 512
    cos4, sin4 = pl.pallas_call(
        _rope_table_kernel,
        grid=(rows // tr,),
        in_specs=[pl.BlockSpec((tr, per_row), lambda i: (i, 0)), _const_spec((per_row, LANES))],
        out_specs=[pl.BlockSpec((tr, LANES), lambda i: (i, 0))] * 2,
        out_shape=[jax.ShapeDtypeStruct((rows, LANES), F32)] * 2,
        compiler_params=_cparams("parallel"),
        name="rope_table",
    )(positions.reshape(rows, per_row), jnp.asarray(e))
    return cos4.reshape(t, half), sin4.reshape(t, half)


def _rope128(x, c, s1, s2):
    return x * c - pltpu.roll(x, LANES - MLA_ROPE // 2, axis=1) * s1 + pltpu.roll(x, MLA_ROPE // 2, axis=1) * s2


def _mla_proj_kernel(h_ref, ng_ref, wa_ref, wz_ref, qg_ref, wuq_ref, kg_ref, wukv_ref, c_ref, s1_ref, s2_ref,
                     z_ref, q_ref, k_ref, v_ref, *, q_rank, kv_rank, scale):
    hn = _rms(h_ref[...], ng_ref[...]).astype(BF16)
    z_ref[...] = _dot(hn, wz_ref[...]).astype(BF16)
    a = _dot(hn, wa_ref[...])
    c, s1, s2 = c_ref[...], s1_ref[...], s2_ref[...]
    cq = _rms(a[:, 0:q_rank], qg_ref[...]).astype(BF16)
    ckv = _rms(a[:, q_rank:q_rank + kv_rank], kg_ref[...]).astype(BF16)
    kr = _rope128(a[:, q_rank + kv_rank:q_rank + kv_rank + LANES], c, s1, s2).astype(BF16)
    q = _dot(cq, wuq_ref[...]) * scale
    kv = _dot(ckv, wukv_ref[...])
    for hd in range(MLA_HEADS):
        o = hd * MLA_QK_PAD
        q_ref[:, o:o + MLA_NOPE] = q[:, o:o + MLA_NOPE].astype(BF16)
        q_ref[:, o + MLA_NOPE:o + MLA_QK_PAD] = _rope128(q[:, o + MLA_NOPE:o + MLA_QK_PAD], c, s1, s2).astype(BF16)
        k_ref[:, o:o + MLA_NOPE] = kv[:, o:o + MLA_NOPE].astype(BF16)
        k_ref[:, o + MLA_NOPE:o + MLA_QK_PAD] = kr
        v_ref[:, hd * MLA_V:(hd + 1) * MLA_V] = kv[:, o + MLA_NOPE:o + MLA_NOPE + MLA_V].astype(BF16)


def _mla_proj(h, norm_g, w_in, q_norm_g, w_uq, kv_norm_g, w_ukv, cos, sin):
    t, d = h.shape
    q_rank, kv_rank = w_uq.shape[0], w_ukv.shape[0]
    n_a = q_rank + kv_rank + MLA_ROPE
    d_inner = w_in.shape[1] - n_a
    tm = ROW_BLOCK
    wa = jnp.pad(w_in[:, :n_a], ((0, 0), (0, LANES - MLA_ROPE))).astype(BF16)
    wz = w_in[:, n_a:].astype(BF16)
    qk = MLA_NOPE + MLA_ROPE
    wuq = jnp.pad(w_uq.reshape(q_rank, MLA_HEADS, qk), ((0, 0), (0, 0), (0, MLA_QK_PAD - qk)))
    wuq = wuq.reshape(q_rank, MLA_HEADS * MLA_QK_PAD).astype(BF16)
    half = MLA_ROPE // 2
    zero = jnp.zeros_like(cos)
    ctab = jnp.concatenate([cos, cos, zero, zero], axis=1)
    s1tab = jnp.concatenate([sin, zero, zero, zero], axis=1)
    s2tab = jnp.concatenate([zero, sin, zero, zero], axis=1)
    assert ctab.shape[1] == LANES and half * 4 == LANES
    row = lambda n: pl.BlockSpec((tm, n), lambda i: (i, 0))
    return pl.pallas_call(
        functools.partial(_mla_proj_kernel, q_rank=q_rank, kv_rank=kv_rank, scale=float(qk) ** -0.5 * math.log2(math.e)),
        grid=(t // tm,),
        in_specs=[row(d), _const_spec((1, d)), _const_spec(wa.shape), _const_spec(wz.shape),
                  _const_spec((1, q_rank)), _const_spec(wuq.shape), _const_spec((1, kv_rank)),
                  _const_spec(w_ukv.shape), row(LANES), row(LANES), row(LANES)],
        out_specs=[row(d_inner), row(MLA_HEADS * MLA_QK_PAD), row(MLA_HEADS * MLA_QK_PAD), row(MLA_HEADS * MLA_V)],
        out_shape=[jax.ShapeDtypeStruct((t, d_inner), BF16),
                   jax.ShapeDtypeStruct((t, MLA_HEADS * MLA_QK_PAD), BF16),
                   jax.ShapeDtypeStruct((t, MLA_HEADS * MLA_QK_PAD), BF16),
                   jax.ShapeDtypeStruct((t, MLA_HEADS * MLA_V), BF16)],
        compiler_params=_cparams("parallel"),
        name="mla_proj",
    )(h, norm_g.reshape(1, d), wa, wz, q_norm_g.reshape(1, q_rank), wuq, kv_norm_g.reshape(1, kv_rank),
      w_ukv.astype(BF16), ctab, s1tab, s2tab)


ATT_K = 256
ATT_CQ = 256
ATT_CHAINS = 16
ATT_Q = ATT_CHAINS * ATT_CQ
ATT_LOOP_TILES = 4
ATT_VROWS = MLA_V + 16


def _attn_kernel(q_ref, k_ref, v_ref, o_ref, vxt_sc, acc_sc, m_sc, *, n_q_tiles):
    qi = pl.program_id(2)
    n_kt = vxt_sc.shape[0]

    @pl.when(qi == 0)
    def _():
        row = lax.broadcasted_iota(jnp.int32, (ATT_VROWS - MLA_V, ATT_K), 0)
        ones = jnp.where(row == 0, 1.0, 0.0).astype(BF16)
        for j in range(n_kt):
            vt = v_ref[j * ATT_K:(j + 1) * ATT_K, :].astype(F32).T
            vxt_sc[j, 0:MLA_V, :] = vt.astype(BF16)
            vxt_sc[j, MLA_V:ATT_VROWS, :] = ones

    m_sc[...] = jnp.full_like(m_sc, NEG_INF)
    acc_sc[...] = jnp.zeros_like(acc_sc)

    def qk(c, j):
        koff = pl.multiple_of(j * ATT_K, ATT_K)
        return _dot_nt(k_ref[pl.ds(koff, ATT_K), :], q_ref[c * ATT_CQ:(c + 1) * ATT_CQ, :])

    def softmax(st, c, mask_off):
        qs = slice(c * ATT_CQ, (c + 1) * ATT_CQ)
        if mask_off is not None:
            kk = lax.broadcasted_iota(jnp.int32, st.shape, 0) + mask_off
            qq = lax.broadcasted_iota(jnp.int32, st.shape, 1)
            st = jnp.where(kk <= qq, st, NEG_INF)
        m_prev = m_sc[:, qs]
        m_new = jnp.maximum(m_prev, jnp.max(st, axis=0, keepdims=True))
        alpha = jnp.exp2(m_prev - m_new)
        pt = jnp.exp2(st - m_new).astype(BF16)
        m_sc[:, qs] = m_new
        return alpha, pt

    def accumulate(c, j, alpha, pt):
        qs = slice(c * ATT_CQ, (c + 1) * ATT_CQ)
        acc_sc[:, qs] = alpha * acc_sc[:, qs] + _dot(vxt_sc[j], pt)

    def run(items):
        ahead = 3
        sts = [qk(c, j) for c, j, _ in items[:ahead]]
        pending = None
        for i, (c, j, mask_off) in enumerate(items):
            if i + ahead < len(items):
                sts.append(qk(items[i + ahead][0], items[i + ahead][1]))
            alpha, pt = softmax(sts[i], c, mask_off)
            if pending is not None:
                accumulate(*pending)
            pending = (c, j, alpha, pt)
        accumulate(*pending)

    tiles_per_step = ATT_Q // ATT_K

    def body(i, carry):
        run([(c, ATT_LOOP_TILES * i + e, None) for e in range(ATT_LOOP_TILES) for c in range(ATT_CHAINS)])
        return carry

    base = qi * tiles_per_step
    if n_q_tiles > 1:
        lax.fori_loop(0, base // ATT_LOOP_TILES, body, 0)
    diag = []
    for e in range(tiles_per_step):
        for c in range(ATT_CHAINS):
            k_lo, k_hi = e * ATT_K, (e + 1) * ATT_K - 1
            q_lo, q_hi = c * ATT_CQ, (c + 1) * ATT_CQ - 1
            if k_lo > q_hi:
                continue
            diag.append((c, base + e, None if k_hi <= q_lo else k_lo - q_lo))
    run(diag)
    acc = acc_sc[...]
    o = acc[0:MLA_V, :] / acc[MLA_V:MLA_V + 1, :]
    o_ref[...] = o.T.astype(o_ref.dtype)


def _attention(q, k, v, bsz):
    t = q.shape[0]
    seq = t // bsz
    assert seq % ATT_Q == 0
    nq = seq // ATT_Q
    qmap = lambda b, h, i: (b * nq + i, h)
    kmap = lambda b, h, i: (b, h)
    return pl.pallas_call(
        functools.partial(_attn_kernel, n_q_tiles=nq),
        grid=(bsz, MLA_HEADS, nq),
        in_specs=[pl.BlockSpec((ATT_Q, MLA_QK_PAD), qmap), pl.BlockSpec((seq, MLA_QK_PAD), kmap),
                  pl.BlockSpec((seq, MLA_V), kmap)],
        out_specs=pl.BlockSpec((ATT_Q, MLA_V), qmap),
        out_shape=jax.ShapeDtypeStruct((t, MLA_HEADS * MLA_V), BF16),
        scratch_shapes=[pltpu.VMEM((seq // ATT_K, ATT_VROWS, ATT_K), BF16), pltpu.VMEM((ATT_VROWS, ATT_Q), F32),
                        pltpu.VMEM((1, ATT_Q), F32)],
        compiler_params=_cparams("parallel", "parallel", "arbitrary"),
        name="mla_attn",
    )(q, k, v)


def _mla_out_kernel(h_ref, o_ref, z_ref, wo_ref, out_ref):
    y = o_ref[...].astype(F32) * _silu(z_ref[...].astype(F32))
    out_ref[...] = h_ref[...] + _dot(y.astype(BF16), wo_ref[...])


def _mla_out(h, o, z, w_out):
    t, d = h.shape
    d_inner = w_out.shape[0]
    tm = ROW_BLOCK
    return pl.pallas_call(
        _mla_out_kernel,
        grid=(t // tm,),
        in_specs=[pl.BlockSpec((tm, d), lambda i: (i, 0)), pl.BlockSpec((tm, d_inner), lambda i: (i, 0)),
                  pl.BlockSpec((tm, d_inner), lambda i: (i, 0)), _const_spec(w_out.shape)],
        out_specs=pl.BlockSpec((tm, d), lambda i: (i, 0)),
        out_shape=jax.ShapeDtypeStruct((t, d), F32),
        compiler_params=_cparams("parallel"),
        name="mla_out",
    )(h, o, z, w_out.astype(BF16))


def _mla_layer(h, bsz, positions, norm_g, w_in, q_norm_g, w_uq, kv_norm_g, w_ukv, w_out):
    cos, sin = _rope_tables(positions)
    z, q, k, v = _mla_proj(h, norm_g, w_in, q_norm_g, w_uq, kv_norm_g, w_ukv, cos, sin)
    o = _attention(q, k, v, bsz)
    return _mla_out(h, o, z, w_out)


def kernel(x, positions, l0_norm_g, l0_w_in, l0_ln_g, l0_ln_b, l0_w_s, l0_b_s, l0_w_out, l1_norm_g, l1_w_in, l1_a_re, l1_a_im, l1_log_step, l1_b_re, l1_b_im, l1_c_re, l1_c_im, l1_d_skip, l1_w_glu, l1_b_glu, l1_w_out, l2_norm_g, l2_w_in, l2_q_norm_g, l2_w_uq, l2_kv_norm_g, l2_w_ukv, l2_w_out, l3_norm_g, l3_w_in, l3_ln_g, l3_ln_b, l3_w_s, l3_b_s, l3_w_out, final_norm_g):
    bsz, seq, d = x.shape
    h = x.reshape(bsz * seq, d)
    h = _gmlp_layer(h, l0_norm_g, l0_w_in, l0_ln_g, l0_ln_b, l0_w_s, l0_b_s, l0_w_out)
    h = _s5_layer(h, bsz, l1_norm_g, l1_w_in, l1_a_re, l1_a_im, l1_log_step, l1_b_re, l1_b_im, l1_c_re, l1_c_im,
                  l1_d_skip, l1_w_glu, l1_b_glu, l1_w_out)
    h = _mla_layer(h, bsz, positions, l2_norm_g, l2_w_in, l2_q_norm_g, l2_w_uq, l2_kv_norm_g, l2_w_ukv, l2_w_out)
    h = _gmlp_layer(h, l3_norm_g, l3_w_in, l3_ln_g, l3_ln_b, l3_w_s, l3_b_s, l3_w_out, final_g=final_norm_g)
    return h.reshape(bsz, seq, d)
```

```python
import functools
import math

import jax
import jax.numpy as jnp
import numpy as np
from jax import lax
from jax.experimental import pallas as pl
from jax.experimental.pallas import tpu as pltpu

F32 = jnp.float32
BF16 = jnp.bfloat16

NORM_EPS = 1e-6
LANES = 128
VMEM_LIMIT = 56 * 1024 * 1024

GMLP_CHUNK = 128
GMLP_GROUPS = 8
S5_GROUP = 16
S5_STATE = 64
S5_Q = 32
MLA_HEADS = 16
MLA_NOPE = 128
MLA_ROPE = 64
MLA_V = 128
MLA_QK_PAD = 256
ROPE_THETA = 10000.0
NEG_INF = -1e30

ROW_BLOCK = 512


def _cparams(*sem):
    return pltpu.CompilerParams(dimension_semantics=sem, vmem_limit_bytes=VMEM_LIMIT)


def _const_spec(shape):
    nd = len(shape)
    return pl.BlockSpec(shape, lambda *_: (0,) * nd, pipeline_mode=pl.Buffered(1))


def _dot(a, b):
    return jnp.dot(a, b, preferred_element_type=F32)


def _dot_nt(a, b, precision=None):
    return lax.dot_general(a, b, (((1,), (1,)), ((), ())), preferred_element_type=F32, precision=precision)


def _rms(x, g):
    return x * lax.rsqrt(jnp.mean(x * x, axis=-1, keepdims=True) + NORM_EPS) * g


def _gelu(x):
    return 0.5 * x * (1.0 + jnp.tanh(math.sqrt(2.0 / math.pi) * (x + 0.044715 * (x * x * x))))


def _sigmoid(x):
    return 0.5 * (1.0 + jnp.tanh(0.5 * x))


def _silu(x):
    return x * _sigmoid(x)


def _gmlp_kernel(h_ref, ng_ref, win_ref, lng_ref, lnb_ref, ws_ref, bst_ref, wout_ref, *rest, d_inner, final):
    if final:
        fg_ref, o_ref = rest
    else:
        (o_ref,) = rest
    x = h_ref[...]
    tm = x.shape[0]
    hn = _rms(x, ng_ref[...]).astype(BF16)
    u = _gelu(_dot(hn, win_ref[:, 0:d_inner]))
    v = _gelu(_dot(hn, win_ref[:, d_inner:2 * d_inner]))
    mu = jnp.mean(v, axis=-1, keepdims=True)
    vc = v - mu
    var = jnp.mean(vc * vc, axis=-1, keepdims=True)
    vn = (vc * lax.rsqrt(var + NORM_EPS) * lng_ref[...] + lnb_ref[...]).astype(BF16)
    z = _dot(hn, win_ref[:, 2 * d_inner:3 * d_inner])
    gate = u * _silu(z)

    gw = d_inner // GMLP_GROUPS
    r = lax.broadcasted_iota(jnp.int32, (GMLP_CHUNK, GMLP_CHUNK), 0)
    c = lax.broadcasted_iota(jnp.int32, (GMLP_CHUNK, GMLP_CHUNK), 1)
    causal = c <= r
    cols = []
    for g in range(GMLP_GROUPS):
        w = jnp.where(causal, ws_ref[g], 0.0).astype(BF16)
        bias = bst_ref[:, g:g + 1]
        rows = []
        for ci in range(tm // GMLP_CHUNK):
            vs = vn[ci * GMLP_CHUNK:(ci + 1) * GMLP_CHUNK, g * gw:(g + 1) * gw]
            rows.append(_dot(w, vs) + bias)
        cols.append(jnp.concatenate(rows, axis=0) if len(rows) > 1 else rows[0])
    s = jnp.concatenate(cols, axis=1)
    y = (gate * s).astype(BF16)
    out = x + _dot(y, wout_ref[...])
    if final:
        out = _rms(out, fg_ref[...])
    o_ref[...] = out


def _gmlp_layer(h, norm_g, w_in, ln_g, ln_b, w_s, b_s, w_out, final_g=None):
    t, d = h.shape
    d_inner = w_out.shape[0]
    tm = ROW_BLOCK
    final = final_g is not None
    args = [h, norm_g.reshape(1, d), w_in.astype(BF16), ln_g.reshape(1, d_inner), ln_b.reshape(1, d_inner),
            w_s, b_s.T, w_out.astype(BF16)]
    specs = [pl.BlockSpec((tm, d), lambda i: (i, 0)), _const_spec((1, d)), _const_spec(w_in.shape),
             _const_spec((1, d_inner)), _const_spec((1, d_inner)), _const_spec(w_s.shape),
             _const_spec((GMLP_CHUNK, GMLP_GROUPS)), _const_spec(w_out.shape)]
    if final:
        args.append(final_g.reshape(1, d))
        specs.append(_const_spec((1, d)))
    return pl.pallas_call(
        functools.partial(_gmlp_kernel, d_inner=d_inner, final=final),
        grid=(t // tm,),
        in_specs=specs,
        out_specs=pl.BlockSpec((tm, d), lambda i: (i, 0)),
        out_shape=jax.ShapeDtypeStruct((t, d), F32),
        compiler_params=_cparams("parallel"),
        name="gmlp_final" if final else "gmlp",
    )(*args)


S5_OCT = LANES // S5_GROUP
S5_ROWS = 512


def _block_transpose8(v):
    blk = lax.broadcasted_iota(jnp.int32, v[0].shape, 1) // S5_GROUP
    for k in range(3):
        d = 1 << k
        low = ((blk >> k) & 1) == 0
        new = list(v)
        for i in range(S5_OCT):
            if (i >> k) & 1 == 0:
                a, b = v[i], v[i + d]
                new[i] = jnp.where(low, a, pltpu.roll(b, S5_GROUP * d, axis=1))
                new[i + d] = jnp.where(low, pltpu.roll(a, LANES - S5_GROUP * d, axis=1), b)
        v = new
    return v


def _s5_row_perm():
    n = 8 * S5_Q
    p = np.zeros((n, n), np.float32)
    for c in range(8):
        for s in range(S5_Q):
            p[s * 8 + c, c * S5_Q + s] = 1.0
    return p


def _s5_in_kernel(h_ref, ng_ref, win_ref, perm_ref, ur_ref, z_ref, u_sc, *, d_inner):
    hn = _rms(h_ref[...], ng_ref[...]).astype(BF16)
    z_ref[...] = _dot(hn, win_ref[:, d_inner:2 * d_inner]).astype(BF16)
    u = _dot(hn, win_ref[:, 0:d_inner]).astype(BF16)
    n_oct = d_inner // LANES
    q = S5_Q
    blk_rows = 8 * q
    halves = S5_ROWS // blk_rows
    for cb in range(halves):
        up = _dot(perm_ref[...], u[cb * blk_rows:(cb + 1) * blk_rows, :])
        for o in range(n_oct):
            u_sc[o, cb * blk_rows:(cb + 1) * blk_rows, :] = up[:, o * LANES:(o + 1) * LANES]

    def octet(o, carry):
        parts = []
        for cb in range(halves):
            row = []
            for sb in range(q // 8):
                rs = [u_sc[o, cb * blk_rows + (sb * 8 + si) * 8:cb * blk_rows + (sb * 8 + si + 1) * 8, :]
                      for si in range(8)]
                row.append(_block_transpose8(rs))
            parts.append(row)
        for gi in range(S5_OCT):
            blk = jnp.concatenate(
                [jnp.concatenate([parts[cb][sb][gi] for sb in range(q // 8)], axis=1) for cb in range(halves)], axis=0)
            ur_ref[o * S5_OCT + gi] = blk.astype(BF16)
        return carry

    lax.fori_loop(0, n_oct, octet, 0)


def _s5_in(h, norm_g, w_in):
    t, d = h.shape
    d_inner = w_in.shape[1] // 2
    g = d_inner // S5_GROUP
    tm = S5_ROWS
    return pl.pallas_call(
        functools.partial(_s5_in_kernel, d_inner=d_inner),
        grid=(t // tm,),
        in_specs=[pl.BlockSpec((tm, d), lambda i: (i, 0)), _const_spec((1, d)), _const_spec(w_in.shape),
                  _const_spec((8 * S5_Q, 8 * S5_Q))],
        out_specs=[pl.BlockSpec((g, tm // S5_Q, S5_Q * S5_GROUP), lambda i: (0, i, 0)),
                   pl.BlockSpec((tm, d_inner), lambda i: (i, 0))],
        out_shape=[jax.ShapeDtypeStruct((g, t // S5_Q, S5_Q * S5_GROUP), BF16),
                   jax.ShapeDtypeStruct((t, d_inner), BF16)],
        scratch_shapes=[pltpu.VMEM((d_inner // LANES, tm, LANES), F32)],
        compiler_params=_cparams("parallel"),
        name="s5_in",
    )(h, norm_g.reshape(1, d), w_in.astype(BF16), jnp.asarray(_s5_row_perm(), BF16))


S5_MAX_LEVELS = 8
S5_PREP_GROUPS = 8
S5_POW_ROWS = 40


def _s5_prep_kernel(are_ref, aim_ref, ls_ref, bx_ref, by_ref, cr_ref, ci_ref,
                    m_ref, p_ref, rt_ref, la_ref, lb_ref, a_sc):
    q, hh = S5_Q, S5_GROUP
    lane = lax.broadcasted_iota(jnp.int32, (1, LANES), 1)
    lo = lane < S5_STATE
    sgn = jnp.where(lo, -1.0, 1.0).astype(F32)

    j = lax.broadcasted_iota(jnp.int32, (S5_POW_ROWS, LANES), 0).astype(F32)

    def group(gi):
        step = jnp.exp(ls_ref[gi])
        ar, ai = are_ref[gi], aim_ref[gi]
        lr, li = ar * step, ai * step
        mag = jnp.exp(j * lr)
        er, ei = mag * jnp.cos(j * li), mag * jnp.sin(j * li)
        nr, ni = er[1:2] - 1.0, ei[1:2]
        den = 1.0 / (ar * ar + ai * ai)
        cfr = (nr * ar + ni * ai) * den
        cfi = (ni * ar - nr * ai) * den

        bx, by = bx_ref[gi], by_ref[gi]
        bba = cfr * bx + cfi * sgn * by
        bbb = sgn * cfr * by - cfi * bx
        bbt = -(sgn * cfr * bx) - cfi * by

        c1 = cr_ref[gi]
        c2 = sgn * ci_ref[gi]
        ee1 = jnp.where(lo, er, ei)
        ee2 = jnp.where(lo, ei, er)
        for jj in range(q + 1):
            a_sc[gi, jj * hh:(jj + 1) * hh, :] = c1 * ee1[jj:jj + 1] + c2 * ee2[jj:jj + 1]

        for s in range(q):
            e = q - 1 - s
            p_ref[gi, s * hh:(s + 1) * hh, :] = (er[e:e + 1] * bba + ei[e:e + 1] * bbb).astype(BF16)

        rt_ref[gi] = (-sgn * a_sc[gi, hh:(q + 1) * hh, :]).astype(BF16)

        kt = _dot_nt(bbt, a_sc[gi, 0:q * hh, :], precision=lax.Precision.HIGHEST)
        col = lax.broadcasted_iota(jnp.int32, kt.shape, 1)
        for s in range(q):
            if s == 0:
                blk = kt
            else:
                blk = jnp.where(col >= s * hh, pltpu.roll(kt, s * hh, axis=1), 0.0)
            m_ref[gi, s * hh:(s + 1) * hh, :] = blk.astype(BF16)

        pr, pi = er[q:q + 1], ei[q:q + 1]
        for k in range(S5_MAX_LEVELS):
            la_ref[gi, k:k + 1, :] = pr
            lb_ref[gi, k:k + 1, :] = sgn * pi
            pr, pi = pr * pr - pi * pi, 2.0 * pr * pi

    for gi in range(are_ref.shape[0]):
        group(gi)


def _s5_prep(a_re, a_im, log_step, b_re, b_im, c_re, c_im):
    g, p = a_re.shape
    hh = b_re.shape[2]
    qh = S5_Q * hh
    dup = lambda x: jnp.concatenate([x, x], axis=-1)
    are2 = dup(a_re).reshape(g, 1, 2 * p)
    aim2 = dup(a_im).reshape(g, 1, 2 * p)
    ls = log_step.reshape(g, 1, 1)
    brt, bit = jnp.swapaxes(b_re, 1, 2), jnp.swapaxes(b_im, 1, 2)
    bx = jnp.concatenate([brt, bit], axis=-1)
    by = jnp.concatenate([bit, brt], axis=-1)
    gb = S5_PREP_GROUPS
    gspec = lambda *s: pl.BlockSpec((gb,) + s, lambda i: (i,) + (0,) * len(s))
    return pl.pallas_call(
        _s5_prep_kernel,
        grid=(g // gb,),
        in_specs=[gspec(1, 2 * p), gspec(1, 2 * p), gspec(1, 1), gspec(hh, 2 * p), gspec(hh, 2 * p),
                  gspec(hh, 2 * p), gspec(hh, 2 * p)],
        out_specs=[gspec(qh, qh), gspec(qh, 2 * p), gspec(qh, 2 * p), gspec(8, 2 * p), gspec(8, 2 * p)],
        out_shape=[jax.ShapeDtypeStruct((g, qh, qh), BF16), jax.ShapeDtypeStruct((g, qh, 2 * p), BF16),
                   jax.ShapeDtypeStruct((g, qh, 2 * p), BF16), jax.ShapeDtypeStruct((g, 8, 2 * p), F32),
                   jax.ShapeDtypeStruct((g, 8, 2 * p), F32)],
        scratch_shapes=[pltpu.VMEM((gb, (S5_Q + 1) * hh, 2 * p), F32)],
        compiler_params=_cparams("parallel"),
        name="s5_prep",
    )(are2, aim2, ls, bx, by, dup(c_re), dup(c_im))


def _s5_core_kernel(u_ref, m_ref, p_ref, rt_ref, la_ref, lb_ref, d_ref, y_ref, yg_sc, *, nc, levels):
    n = u_ref.shape[1]
    q = S5_Q
    cidx = lax.broadcasted_iota(jnp.int32, (n, LANES), 0) & (nc - 1)
    for gi in range(S5_OCT):
        u = u_ref[gi]
        z = _dot(u, p_ref[gi])
        la, lb = la_ref[gi], lb_ref[gi]
        for k in range(levels):
            sh = 1 << k
            zs = jnp.where(cidx >= sh, pltpu.roll(z, sh, axis=0), 0.0)
            zw = pltpu.roll(zs, S5_STATE, axis=1)
            z = z + la[k:k + 1] * zs + lb[k:k + 1] * zw
        x = jnp.where(cidx >= 1, pltpu.roll(z, 1, axis=0), 0.0)
        y = _dot(u, m_ref[gi]) + _dot_nt(x.astype(BF16), rt_ref[gi]) + u.astype(F32) * d_ref[gi]
        yg_sc[gi] = y

    def rowgroup(rg, carry):
        r0 = pl.multiple_of(rg * 8, 8)
        out0 = pl.multiple_of(rg * 8 * q, 8 * q)
        for tb in range(q // 8):
            blocks = [_gelu(yg_sc[gi, pl.ds(r0, 8), tb * LANES:(tb + 1) * LANES]) for gi in range(S5_OCT)]
            rows = _block_transpose8(blocks)
            for ti in range(0, 8, 2):
                pair = jnp.concatenate([rows[ti], rows[ti + 1]], axis=0).astype(BF16)
                y_ref[pl.ds(out0 + (tb * 8 + ti) * 8, 16), :] = pair
        return carry

    lax.fori_loop(0, n // 8, rowgroup, 0, unroll=2)


def _s5_core(ur, m, p, rt, la, lb, dt, nc):
    g, n, qh = ur.shape
    levels = (nc - 1).bit_length()
    assert nc == 1 << levels and levels <= S5_MAX_LEVELS and n % nc == 0 and qh == S5_Q * S5_GROUP
    ospec = lambda *s: pl.BlockSpec((S5_OCT,) + s, lambda i: (i,) + (0,) * len(s))
    return pl.pallas_call(
        functools.partial(_s5_core_kernel, nc=nc, levels=levels),
        grid=(g // S5_OCT,),
        in_specs=[ospec(n, qh), ospec(qh, qh), ospec(qh, LANES), ospec(qh, LANES), ospec(8, LANES),
                  ospec(8, LANES), ospec(1, qh)],
        out_specs=pl.BlockSpec((n * S5_Q, LANES), lambda i: (0, i)),
        out_shape=jax.ShapeDtypeStruct((n * S5_Q, g * S5_GROUP), BF16),
        scratch_shapes=[pltpu.VMEM((S5_OCT, n, qh), F32)],
        compiler_params=_cparams("parallel"),
        name="s5_core",
    )(ur, m, p, rt, la, lb, dt)


def _s5_out_kernel(h_ref, y_ref, z_ref, permt_ref, wg_ref, bg_ref, wo_ref, o_ref):
    blk = permt_ref.shape[0]
    y = jnp.concatenate([_dot(permt_ref[...], y_ref[r:r + blk, :]) for r in range(0, y_ref.shape[0], blk)],
                        axis=0).astype(BF16)
    g = _dot(y, wg_ref[...]) + bg_ref[...]
    y2 = y.astype(F32) * _sigmoid(g) * _silu(z_ref[...].astype(F32))
    o_ref[...] = h_ref[...] + _dot(y2.astype(BF16), wo_ref[...])


def _s5_out(h, y, z, w_glu, b_glu, w_out):
    t, d = h.shape
    d_inner = w_out.shape[0]
    blk = 8 * S5_Q
    tm = ROW_BLOCK
    assert tm % blk == 0
    return pl.pallas_call(
        _s5_out_kernel,
        grid=(t // tm,),
        in_specs=[pl.BlockSpec((tm, d), lambda i: (i, 0)), pl.BlockSpec((tm, d_inner), lambda i: (i, 0)),
                  pl.BlockSpec((tm, d_inner), lambda i: (i, 0)), _const_spec((blk, blk)), _const_spec(w_glu.shape),
                  _const_spec((1, d_inner)), _const_spec(w_out.shape)],
        out_specs=pl.BlockSpec((tm, d), lambda i: (i, 0)),
        out_shape=jax.ShapeDtypeStruct((t, d), F32),
        compiler_params=_cparams("parallel"),
        name="s5_out",
    )(h, y, z, jnp.asarray(_s5_row_perm().T, BF16), w_glu.astype(BF16), b_glu.reshape(1, d_inner),
      w_out.astype(BF16))


def _s5_layer(h, bsz, norm_g, w_in, a_re, a_im, log_step, b_re, b_im, c_re, c_im, d_skip, w_glu, b_glu, w_out):
    t, _ = h.shape
    seq = t // bsz
    g, _ = a_re.shape
    hh = b_re.shape[2]
    q = S5_Q
    nc = seq // q
    assert hh == S5_GROUP and seq % S5_ROWS == 0
    ur, z = _s5_in(h, norm_g, w_in)
    m, p, rt, la, lb = _s5_prep(a_re, a_im, log_step, b_re, b_im, c_re, c_im)
    dt = jnp.tile(d_skip.reshape(g, 1, hh), (1, 1, q))
    y = _s5_core(ur, m, p, rt, la, lb, dt, nc)
    return _s5_out(h, y, z, w_glu, b_glu, w_out)


def _rope_table_kernel(pos_ref, e_ref, cos_ref, sin_ref):
    p = pos_ref[...].astype(F32)
    ang = p[:, 0:1] * e_ref[0:1, :]
    for k in range(1, 4):
        ang = ang + p[:, k:k + 1] * e_ref[k:k + 1, :]
    cos_ref[...] = jnp.cos(ang)
    sin_ref[...] = jnp.sin(ang)


def _rope_tables(positions):
    t = positions.size
    half = MLA_ROPE // 2
    per_row = LANES // half
    inv_freq = (ROPE_THETA ** (-np.arange(0, MLA_ROPE, 2, dtype=np.float32) / np.float32(MLA_ROPE))).astype(np.float32)
    e = np.zeros((per_row, LANES), np.float32)
    for k in range(per_row):
        e[k, k * half:(k + 1) * half] = inv_freq
    rows = t // per_row
    tr = 512
    cos4, sin4 = pl.pallas_call(
        _rope_table_kernel,
        grid=(rows // tr,),
        in_specs=[pl.BlockSpec((tr, per_row), lambda i: (i, 0)), _const_spec((per_row, LANES))],
        out_specs=[pl.BlockSpec((tr, LANES), lambda i: (i, 0))] * 2,
        out_shape=[jax.ShapeDtypeStruct((rows, LANES), F32)] * 2,
        compiler_params=_cparams("parallel"),
        name="rope_table",
    )(positions.reshape(rows, per_row), jnp.asarray(e))
    return cos4.reshape(t, half), sin4.reshape(t, half)


def _rope128(x, c, s1, s2):
    return x * c - pltpu.roll(x, LANES - MLA_ROPE // 2, axis=1) * s1 + pltpu.roll(x, MLA_ROPE // 2, axis=1) * s2


def _mla_proj_kernel(h_ref, ng_ref, wa_ref, wz_ref, qg_ref, wuq_ref, kg_ref, wukv_ref, c_ref, s1_ref, s2_ref,
                     z_ref, q_ref, k_ref, v_ref, *, q_rank, kv_rank, scale):
    hn = _rms(h_ref[...], ng_ref[...]).astype(BF16)
    z_ref[...] = _dot(hn, wz_ref[...]).astype(BF16)
    a = _dot(hn, wa_ref[...])
    c, s1, s2 = c_ref[...], s1_ref[...], s2_ref[...]
    cq = _rms(a[:, 0:q_rank], qg_ref[...]).astype(BF16)
    ckv = _rms(a[:, q_rank:q_rank + kv_rank], kg_ref[...]).astype(BF16)
    kr = _rope128(a[:, q_rank + kv_rank:q_rank + kv_rank + LANES], c, s1, s2).astype(BF16)
    q = _dot(cq, wuq_ref[...]) * scale
    kv = _dot(ckv, wukv_ref[...])
    for hd in range(MLA_HEADS):
        o = hd * MLA_QK_PAD
        q_ref[:, o:o + MLA_NOPE] = q[:, o:o + MLA_NOPE].astype(BF16)
        q_ref[:, o + MLA_NOPE:o + MLA_QK_PAD] = _rope128(q[:, o + MLA_NOPE:o + MLA_QK_PAD], c, s1, s2).astype(BF16)
        k_ref[:, o:o + MLA_NOPE] = kv[:, o:o + MLA_NOPE].astype(BF16)
        k_ref[:, o + MLA_NOPE:o + MLA_QK_PAD] = kr
        v_ref[:, hd * MLA_V:(hd + 1) * MLA_V] = kv[:, o + MLA_NOPE:o + MLA_NOPE + MLA_V].astype(BF16)


def _mla_proj(h, norm_g, w_in, q_norm_g, w_uq, kv_norm_g, w_ukv, cos, sin):
    t, d = h.shape
    q_rank, kv_rank = w_uq.shape[0], w_ukv.shape[0]
    n_a = q_rank + kv_rank + MLA_ROPE
    d_inner = w_in.shape[1] - n_a
    tm = ROW_BLOCK
    wa = jnp.pad(w_in[:, :n_a], ((0, 0), (0, LANES - MLA_ROPE))).astype(BF16)
    wz = w_in[:, n_a:].astype(BF16)
    qk = MLA_NOPE + MLA_ROPE
    wuq = jnp.pad(w_uq.reshape(q_rank, MLA_HEADS, qk), ((0, 0), (0, 0), (0, MLA_QK_PAD - qk)))
    wuq = wuq.reshape(q_rank, MLA_HEADS * MLA_QK_PAD).astype(BF16)
    half = MLA_ROPE // 2
    zero = jnp.zeros_like(cos)
    ctab = jnp.concatenate([cos, cos, zero, zero], axis=1)
    s1tab = jnp.concatenate([sin, zero, zero, zero], axis=1)
    s2tab = jnp.concatenate([zero, sin, zero, zero], axis=1)
    assert ctab.shape[1] == LANES and half * 4 == LANES
    row = lambda n: pl.BlockSpec((tm, n), lambda i: (i, 0))
    return pl.pallas_call(
        functools.partial(_mla_proj_kernel, q_rank=q_rank, kv_rank=kv_rank, scale=float(qk) ** -0.5 * math.log2(math.e)),
        grid=(t // tm,),
        in_specs=[row(d), _const_spec((1, d)), _const_spec(wa.shape), _const_spec(wz.shape),
                  _const_spec((1, q_rank)), _const_spec(wuq.shape), _const_spec((1, kv_rank)),
                  _const_spec(w_ukv.shape), row(LANES), row(LANES), row(LANES)],
        out_specs=[row(d_inner), row(MLA_HEADS * MLA_QK_PAD), row(MLA_HEADS * MLA_QK_PAD), row(MLA_HEADS * MLA_V)],
        out_shape=[jax.ShapeDtypeStruct((t, d_inner), BF16),
                   jax.ShapeDtypeStruct((t, MLA_HEADS * MLA_QK_PAD), BF16),
                   jax.ShapeDtypeStruct((t, MLA_HEADS * MLA_QK_PAD), BF16),
                   jax.ShapeDtypeStruct((t, MLA_HEADS * MLA_V), BF16)],
        compiler_params=_cparams("parallel"),
        name="mla_proj",
    )(h, norm_g.reshape(1, d), wa, wz, q_norm_g.reshape(1, q_rank), wuq, kv_norm_g.reshape(1, kv_rank),
      w_ukv.astype(BF16), ctab, s1tab, s2tab)


ATT_K = 256
ATT_CQ = 256
ATT_CHAINS = 16
ATT_Q = ATT_CHAINS * ATT_CQ
ATT_LOOP_TILES = 4
ATT_VROWS = MLA_V + 16


def _attn_kernel(q_ref, k_ref, v_ref, o_ref, vxt_sc, acc_sc, m_sc, *, n_q_tiles):
    qi = pl.program_id(2)
    n_kt = vxt_sc.shape[0]

    @pl.when(qi == 0)
    def _():
        row = lax.broadcasted_iota(jnp.int32, (ATT_VROWS - MLA_V, ATT_K), 0)
        ones = jnp.where(row == 0, 1.0, 0.0).astype(BF16)
        for j in range(n_kt):
            vt = v_ref[j * ATT_K:(j + 1) * ATT_K, :].astype(F32).T
            vxt_sc[j, 0:MLA_V, :] = vt.astype(BF16)
            vxt_sc[j, MLA_V:ATT_VROWS, :] = ones

    m_sc[...] = jnp.full_like(m_sc, NEG_INF)
    acc_sc[...] = jnp.zeros_like(acc_sc)

    def qk(c, j):
        koff = pl.multiple_of(j * ATT_K, ATT_K)
        return _dot_nt(k_ref[pl.ds(koff, ATT_K), :], q_ref[c * ATT_CQ:(c + 1) * ATT_CQ, :])

    def softmax(st, c, mask_off):
        qs = slice(c * ATT_CQ, (c + 1) * ATT_CQ)
        if mask_off is not None:
            kk = lax.broadcasted_iota(jnp.int32, st.shape, 0) + mask_off
            qq = lax.broadcasted_iota(jnp.int32, st.shape, 1)
            st = jnp.where(kk <= qq, st, NEG_INF)
        m_prev = m_sc[:, qs]
        m_new = jnp.maximum(m_prev, jnp.max(st, axis=0, keepdims=True))
        alpha = jnp.exp2(m_prev - m_new)
        pt = jnp.exp2(st - m_new).astype(BF16)
        m_sc[:, qs] = m_new
        return alpha, pt

    def accumulate(c, j, alpha, pt):
        qs = slice(c * ATT_CQ, (c + 1) * ATT_CQ)
        acc_sc[:, qs] = alpha * acc_sc[:, qs] + _dot(vxt_sc[j], pt)

    def run(items):
        ahead = 4
        sts = [qk(c, j) for c, j, _ in items[:ahead]]
        pending = None
        for i, (c, j, mask_off) in enumerate(items):
            if i + ahead < len(items):
                sts.append(qk(items[i + ahead][0], items[i + ahead][1]))
            alpha, pt = softmax(sts[i], c, mask_off)
            if pending is not None:
                accumulate(*pending)
            pending = (c, j, alpha, pt)
        accumulate(*pending)

    tiles_per_step = ATT_Q // ATT_K

    def body(i, carry):
        run([(c, ATT_LOOP_TILES * i + e, None) for e in range(ATT_LOOP_TILES) for c in range(ATT_CHAINS)])
        return carry

    base = qi * tiles_per_step
    if n_q_tiles > 1:
        lax.fori_loop(0, base // ATT_LOOP_TILES, body, 0)
    diag = []
    for e in range(tiles_per_step):
        for c in range(ATT_CHAINS):
            k_lo, k_hi = e * ATT_K, (e + 1) * ATT_K - 1
            q_lo, q_hi = c * ATT_CQ, (c + 1) * ATT_CQ - 1
            if k_lo > q_hi:
                continue
            diag.append((c, base + e, None if k_hi <= q_lo else k_lo - q_lo))
    run(diag)
    acc = acc_sc[...]
    o = acc[0:MLA_V, :] / acc[MLA_V:MLA_V + 1, :]
    o_ref[...] = o.T.astype(o_ref.dtype)


def _attention(q, k, v, bsz):
    t = q.shape[0]
    seq = t // bsz
    assert seq % ATT_Q == 0
    nq = seq // ATT_Q
    qmap = lambda b, h, i: (b * nq + i, h)
    kmap = lambda b, h, i: (b, h)
    return pl.pallas_call(
        functools.partial(_attn_kernel, n_q_tiles=nq),
        grid=(bsz, MLA_HEADS, nq),
        in_specs=[pl.BlockSpec((ATT_Q, MLA_QK_PAD), qmap), pl.BlockSpec((seq, MLA_QK_PAD), kmap),
                  pl.BlockSpec((seq, MLA_V), kmap)],
        out_specs=pl.BlockSpec((ATT_Q, MLA_V), qmap),
        out_shape=jax.ShapeDtypeStruct((t, MLA_HEADS * MLA_V), BF16),
        scratch_shapes=[pltpu.VMEM((seq // ATT_K, ATT_VROWS, ATT_K), BF16), pltpu.VMEM((ATT_VROWS, ATT_Q), F32),
                        pltpu.VMEM((1, ATT_Q), F32)],
        compiler_params=_cparams("parallel", "parallel", "arbitrary"),
        name="mla_attn",
    )(q, k, v)


def _mla_out_kernel(h_ref, o_ref, z_ref, wo_ref, out_ref):
    y = o_ref[...].astype(F32) * _silu(z_ref[...].astype(F32))
    out_ref[...] = h_ref[...] + _dot(y.astype(BF16), wo_ref[...])


def _mla_out(h, o, z, w_out):
    t, d = h.shape
    d_inner = w_out.shape[0]
    tm = ROW_BLOCK
    return pl.pallas_call(
        _mla_out_kernel,
        grid=(t // tm,),
        in_specs=[pl.BlockSpec((tm, d), lambda i: (i, 0)), pl.BlockSpec((tm, d_inner), lambda i: (i, 0)),
                  pl.BlockSpec((tm, d_inner), lambda i: (i, 0)), _const_spec(w_out.shape)],
        out_specs=pl.BlockSpec((tm, d), lambda i: (i, 0)),
        out_shape=jax.ShapeDtypeStruct((t, d), F32),
        compiler_params=_cparams("parallel"),
        name="mla_out",
    )(h, o, z, w_out.astype(BF16))


def _mla_layer(h, bsz, positions, norm_g, w_in, q_norm_g, w_uq, kv_norm_g, w_ukv, w_out):
    cos, sin = _rope_tables(positions)
    z, q, k, v = _mla_proj(h, norm_g, w_in, q_norm_g, w_uq, kv_norm_g, w_ukv, cos, sin)
    o = _attention(q, k, v, bsz)
    return _mla_out(h, o, z, w_out)


def kernel(x, positions, l0_norm_g, l0_w_in, l0_ln_g, l0_ln_b, l0_w_s, l0_b_s, l0_w_out, l1_norm_g, l1_w_in, l1_a_re, l1_a_im, l1_log_step, l1_b_re, l1_b_im, l1_c_re, l1_c_im, l1_d_skip, l1_w_glu, l1_b_glu, l1_w_out, l2_norm_g, l2_w_in, l2_q_norm_g, l2_w_uq, l2_kv_norm_g, l2_w_ukv, l2_w_out, l3_norm_g, l3_w_in, l3_ln_g, l3_ln_b, l3_w_s, l3_b_s, l3_w_out, final_norm_g):
    bsz, seq, d = x.shape
    h = x.reshape(bsz * seq, d)
    h = _gmlp_layer(h, l0_norm_g, l0_w_in, l0_ln_g, l0_ln_b, l0_w_s, l0_b_s, l0_w_out)
    h = _s5_layer(h, bsz, l1_norm_g, l1_w_in, l1_a_re, l1_a_im, l1_log_step, l1_b_re, l1_b_im, l1_c_re, l1_c_im,
                  l1_d_skip, l1_w_glu, l1_b_glu, l1_w_out)
    h = _mla_layer(h, bsz, positions, l2_norm_g, l2_w_in, l2_q_norm_g, l2_w_uq, l2_kv_norm_g, l2_w_ukv, l2_w_out)
    h = _gmlp_layer(h, l3_norm_g, l3_w_in, l3_ln_g, l3_ln_b, l3_w_s, l3_b_s, l3_w_out, final_g=final_norm_g)
    return h.reshape(bsz, seq, d)
```

```python
import functools
import math

import jax
import jax.numpy as jnp
import numpy as np
from jax import lax
from jax.experimental import pallas as pl
from jax.experimental.pallas import tpu as pltpu

F32 = jnp.float32
BF16 = jnp.bfloat16

NORM_EPS = 1e-6
LANES = 128
VMEM_LIMIT = 56 * 1024 * 1024

GMLP_CHUNK = 128
GMLP_GROUPS = 8
S5_GROUP = 16
S5_STATE = 64
S5_Q = 32
MLA_HEADS = 16
MLA_NOPE = 128
MLA_ROPE = 64
MLA_V = 128
MLA_QK_PAD = 256
ROPE_THETA = 10000.0
NEG_INF = -1e30

ROW_BLOCK = 512


def _cparams(*sem):
    return pltpu.CompilerParams(dimension_semantics=sem, vmem_limit_bytes=VMEM_LIMIT)


def _const_spec(shape):
    nd = len(shape)
    return pl.BlockSpec(shape, lambda *_: (0,) * nd, pipeline_mode=pl.Buffered(1))


def _dot(a, b):
    return jnp.dot(a, b, preferred_element_type=F32)


def _dot_nt(a, b, precision=None):
    return lax.dot_general(a, b, (((1,), (1,)), ((), ())), preferred_element_type=F32, precision=precision)


def _rms(x, g):
    return x * lax.rsqrt(jnp.mean(x * x, axis=-1, keepdims=True) + NORM_EPS) * g


def _gelu(x):
    return 0.5 * x * (1.0 + jnp.tanh(math.sqrt(2.0 / math.pi) * (x + 0.044715 * (x * x * x))))


def _sigmoid(x):
    return 0.5 * (1.0 + jnp.tanh(0.5 * x))


def _silu(x):
    return x * _sigmoid(x)


def _gmlp_kernel(h_ref, ng_ref, win_ref, lng_ref, lnb_ref, ws_ref, bst_ref, wout_ref, *rest, d_inner, final, fused_in):
    rest = list(rest)
    o_ref = rest.pop()
    fg_ref = rest.pop() if final else None
    x = h_ref[...]
    if fused_in:
        po_ref, pz_ref, pw_ref = rest
        x = x + _dot((po_ref[...].astype(F32) * _silu(pz_ref[...].astype(F32))).astype(BF16), pw_ref[...])
    tm = x.shape[0]
    hn = _rms(x, ng_ref[...]).astype(BF16)
    u = _gelu(_dot(hn, win_ref[:, 0:d_inner]))
    v = _gelu(_dot(hn, win_ref[:, d_inner:2 * d_inner]))
    mu = jnp.mean(v, axis=-1, keepdims=True)
    vc = v - mu
    var = jnp.mean(vc * vc, axis=-1, keepdims=True)
    vn = (vc * lax.rsqrt(var + NORM_EPS) * lng_ref[...] + lnb_ref[...]).astype(BF16)
    z = _dot(hn, win_ref[:, 2 * d_inner:3 * d_inner])
    gate = u * _silu(z)

    gw = d_inner // GMLP_GROUPS
    r = lax.broadcasted_iota(jnp.int32, (GMLP_CHUNK, GMLP_CHUNK), 0)
    c = lax.broadcasted_iota(jnp.int32, (GMLP_CHUNK, GMLP_CHUNK), 1)
    causal = c <= r
    cols = []
    for g in range(GMLP_GROUPS):
        w = jnp.where(causal, ws_ref[g], 0.0).astype(BF16)
        bias = bst_ref[:, g:g + 1]
        rows = []
        for ci in range(tm // GMLP_CHUNK):
            vs = vn[ci * GMLP_CHUNK:(ci + 1) * GMLP_CHUNK, g * gw:(g + 1) * gw]
            rows.append(_dot(w, vs) + bias)
        cols.append(jnp.concatenate(rows, axis=0) if len(rows) > 1 else rows[0])
    s = jnp.concatenate(cols, axis=1)
    y = (gate * s).astype(BF16)
    out = x + _dot(y, wout_ref[...])
    if final:
        out = _rms(out, fg_ref[...])
    o_ref[...] = out


def _gmlp_layer(h, norm_g, w_in, ln_g, ln_b, w_s, b_s, w_out, final_g=None, pre=None):
    t, d = h.shape
    d_inner = w_out.shape[0]
    tm = ROW_BLOCK
    final = final_g is not None
    args = [h, norm_g.reshape(1, d), w_in.astype(BF16), ln_g.reshape(1, d_inner), ln_b.reshape(1, d_inner),
            w_s, b_s.T, w_out.astype(BF16)]
    specs = [pl.BlockSpec((tm, d), lambda i: (i, 0)), _const_spec((1, d)), _const_spec(w_in.shape),
             _const_spec((1, d_inner)), _const_spec((1, d_inner)), _const_spec(w_s.shape),
             _const_spec((GMLP_CHUNK, GMLP_GROUPS)), _const_spec(w_out.shape)]
    if pre is not None:
        po, pz, pw = pre
        args += [po, pz, pw.astype(BF16)]
        specs += [pl.BlockSpec((tm, po.shape[1]), lambda i: (i, 0)), pl.BlockSpec((tm, pz.shape[1]), lambda i: (i, 0)),
                  _const_spec(pw.shape)]
    if final:
        args.append(final_g.reshape(1, d))
        specs.append(_const_spec((1, d)))
    return pl.pallas_call(
        functools.partial(_gmlp_kernel, d_inner=d_inner, final=final, fused_in=pre is not None),
        grid=(t // tm,),
        in_specs=specs,
        out_specs=pl.BlockSpec((tm, d), lambda i: (i, 0)),
        out_shape=jax.ShapeDtypeStruct((t, d), F32),
        compiler_params=_cparams("parallel"),
        name="gmlp_final" if final else "gmlp",
    )(*args)


S5_OCT = LANES // S5_GROUP
S5_ROWS = 512


def _block_transpose8(v):
    blk = lax.broadcasted_iota(jnp.int32, v[0].shape, 1) // S5_GROUP
    for k in range(3):
        d = 1 << k
        low = ((blk >> k) & 1) == 0
        new = list(v)
        for i in range(S5_OCT):
            if (i >> k) & 1 == 0:
                a, b = v[i], v[i + d]
                new[i] = jnp.where(low, a, pltpu.roll(b, S5_GROUP * d, axis=1))
                new[i + d] = jnp.where(low, pltpu.roll(a, LANES - S5_GROUP * d, axis=1), b)
        v = new
    return v


def _s5_row_perm():
    n = 8 * S5_Q
    p = np.zeros((n, n), np.float32)
    for c in range(8):
        for s in range(S5_Q):
            p[s * 8 + c, c * S5_Q + s] = 1.0
    return p


def _s5_in_kernel(h_ref, ng_ref, win_ref, perm_ref, ur_ref, z_ref, u_sc, *, d_inner):
    hn = _rms(h_ref[...], ng_ref[...]).astype(BF16)
    z_ref[...] = _dot(hn, win_ref[:, d_inner:2 * d_inner]).astype(BF16)
    u = _dot(hn, win_ref[:, 0:d_inner]).astype(BF16)
    n_oct = d_inner // LANES
    q = S5_Q
    blk_rows = 8 * q
    halves = S5_ROWS // blk_rows
    for cb in range(halves):
        up = _dot(perm_ref[...], u[cb * blk_rows:(cb + 1) * blk_rows, :])
        for o in range(n_oct):
            u_sc[o, cb * blk_rows:(cb + 1) * blk_rows, :] = up[:, o * LANES:(o + 1) * LANES]

    def octet(o, carry):
        parts = []
        for cb in range(halves):
            row = []
            for sb in range(q // 8):
                rs = [u_sc[o, cb * blk_rows + (sb * 8 + si) * 8:cb * blk_rows + (sb * 8 + si + 1) * 8, :]
                      for si in range(8)]
                row.append(_block_transpose8(rs))
            parts.append(row)
        for gi in range(S5_OCT):
            blk = jnp.concatenate(
                [jnp.concatenate([parts[cb][sb][gi] for sb in range(q // 8)], axis=1) for cb in range(halves)], axis=0)
            ur_ref[o * S5_OCT + gi] = blk.astype(BF16)
        return carry

    lax.fori_loop(0, n_oct, octet, 0)


def _s5_in(h, norm_g, w_in):
    t, d = h.shape
    d_inner = w_in.shape[1] // 2
    g = d_inner // S5_GROUP
    tm = S5_ROWS
    return pl.pallas_call(
        functools.partial(_s5_in_kernel, d_inner=d_inner),
        grid=(t // tm,),
        in_specs=[pl.BlockSpec((tm, d), lambda i: (i, 0)), _const_spec((1, d)), _const_spec(w_in.shape),
                  _const_spec((8 * S5_Q, 8 * S5_Q))],
        out_specs=[pl.BlockSpec((g, tm // S5_Q, S5_Q * S5_GROUP), lambda i: (0, i, 0)),
                   pl.BlockSpec((tm, d_inner), lambda i: (i, 0))],
        out_shape=[jax.ShapeDtypeStruct((g, t // S5_Q, S5_Q * S5_GROUP), BF16),
                   jax.ShapeDtypeStruct((t, d_inner), BF16)],
        scratch_shapes=[pltpu.VMEM((d_inner // LANES, tm, LANES), F32)],
        compiler_params=_cparams("parallel"),
        name="s5_in",
    )(h, norm_g.reshape(1, d), w_in.astype(BF16), jnp.asarray(_s5_row_perm(), BF16))


S5_MAX_LEVELS = 8
S5_PREP_GROUPS = 8
S5_POW_ROWS = 40


def _s5_prep_kernel(are_ref, aim_ref, ls_ref, bx_ref, by_ref, cr_ref, ci_ref,
                    m_ref, p_ref, rt_ref, la_ref, lb_ref, a_sc):
    q, hh = S5_Q, S5_GROUP
    lane = lax.broadcasted_iota(jnp.int32, (1, LANES), 1)
    lo = lane < S5_STATE
    sgn = jnp.where(lo, -1.0, 1.0).astype(F32)

    j = lax.broadcasted_iota(jnp.int32, (S5_POW_ROWS, LANES), 0).astype(F32)

    def group(gi):
        step = jnp.exp(ls_ref[gi])
        ar, ai = are_ref[gi], aim_ref[gi]
        lr, li = ar * step, ai * step
        mag = jnp.exp(j * lr)
        er, ei = mag * jnp.cos(j * li), mag * jnp.sin(j * li)
        nr, ni = er[1:2] - 1.0, ei[1:2]
        den = 1.0 / (ar * ar + ai * ai)
        cfr = (nr * ar + ni * ai) * den
        cfi = (ni * ar - nr * ai) * den

        bx, by = bx_ref[gi], by_ref[gi]
        bba = cfr * bx + cfi * sgn * by
        bbb = sgn * cfr * by - cfi * bx
        bbt = -(sgn * cfr * bx) - cfi * by

        c1 = cr_ref[gi]
        c2 = sgn * ci_ref[gi]
        ee1 = jnp.where(lo, er, ei)
        ee2 = jnp.where(lo, ei, er)
        for jj in range(q + 1):
            a_sc[gi, jj * hh:(jj + 1) * hh, :] = c1 * ee1[jj:jj + 1] + c2 * ee2[jj:jj + 1]

        for s in range(q):
            e = q - 1 - s
            p_ref[gi, s * hh:(s + 1) * hh, :] = (er[e:e + 1] * bba + ei[e:e + 1] * bbb).astype(BF16)

        rt_ref[gi] = (-sgn * a_sc[gi, hh:(q + 1) * hh, :]).astype(BF16)

        kt = _dot_nt(bbt, a_sc[gi, 0:q * hh, :], precision=lax.Precision.HIGHEST)
        col = lax.broadcasted_iota(jnp.int32, kt.shape, 1)
        for s in range(q):
            if s == 0:
                blk = kt
            else:
                blk = jnp.where(col >= s * hh, pltpu.roll(kt, s * hh, axis=1), 0.0)
            m_ref[gi, s * hh:(s + 1) * hh, :] = blk.astype(BF16)

        pr, pi = er[q:q + 1], ei[q:q + 1]
        for k in range(S5_MAX_LEVELS):
            la_ref[gi, k:k + 1, :] = pr
            lb_ref[gi, k:k + 1, :] = sgn * pi
            pr, pi = pr * pr - pi * pi, 2.0 * pr * pi

    for gi in range(are_ref.shape[0]):
        group(gi)


def _s5_prep(a_re, a_im, log_step, b_re, b_im, c_re, c_im):
    g, p = a_re.shape
    hh = b_re.shape[2]
    qh = S5_Q * hh
    dup = lambda x: jnp.concatenate([x, x], axis=-1)
    are2 = dup(a_re).reshape(g, 1, 2 * p)
    aim2 = dup(a_im).reshape(g, 1, 2 * p)
    ls = log_step.reshape(g, 1, 1)
    brt, bit = jnp.swapaxes(b_re, 1, 2), jnp.swapaxes(b_im, 1, 2)
    bx = jnp.concatenate([brt, bit], axis=-1)
    by = jnp.concatenate([bit, brt], axis=-1)
    gb = S5_PREP_GROUPS
    gspec = lambda *s: pl.BlockSpec((gb,) + s, lambda i: (i,) + (0,) * len(s))
    return pl.pallas_call(
        _s5_prep_kernel,
        grid=(g // gb,),
        in_specs=[gspec(1, 2 * p), gspec(1, 2 * p), gspec(1, 1), gspec(hh, 2 * p), gspec(hh, 2 * p),
                  gspec(hh, 2 * p), gspec(hh, 2 * p)],
        out_specs=[gspec(qh, qh), gspec(qh, 2 * p), gspec(qh, 2 * p), gspec(8, 2 * p), gspec(8, 2 * p)],
        out_shape=[jax.ShapeDtypeStruct((g, qh, qh), BF16), jax.ShapeDtypeStruct((g, qh, 2 * p), BF16),
                   jax.ShapeDtypeStruct((g, qh, 2 * p), BF16), jax.ShapeDtypeStruct((g, 8, 2 * p), F32),
                   jax.ShapeDtypeStruct((g, 8, 2 * p), F32)],
        scratch_shapes=[pltpu.VMEM((gb, (S5_Q + 1) * hh, 2 * p), F32)],
        compiler_params=_cparams("parallel"),
        name="s5_prep",
    )(are2, aim2, ls, bx, by, dup(c_re), dup(c_im))


def _s5_core_kernel(u_ref, m_ref, p_ref, rt_ref, la_ref, lb_ref, d_ref, y_ref, yg_sc, *, nc, levels):
    n = u_ref.shape[1]
    q = S5_Q
    cidx = lax.broadcasted_iota(jnp.int32, (n, LANES), 0) & (nc - 1)
    for gi in range(S5_OCT):
        u = u_ref[gi]
        z = _dot(u, p_ref[gi])
        la, lb = la_ref[gi], lb_ref[gi]
        for k in range(levels):
            sh = 1 << k
            zs = jnp.where(cidx >= sh, pltpu.roll(z, sh, axis=0), 0.0)
            zw = pltpu.roll(zs, S5_STATE, axis=1)
            z = z + la[k:k + 1] * zs + lb[k:k + 1] * zw
        x = jnp.where(cidx >= 1, pltpu.roll(z, 1, axis=0), 0.0)
        y = _dot(u, m_ref[gi]) + _dot_nt(x.astype(BF16), rt_ref[gi]) + u.astype(F32) * d_ref[gi]
        yg_sc[gi] = y

    def rowgroup(rg, carry):
        r0 = pl.multiple_of(rg * 8, 8)
        out0 = pl.multiple_of(rg * 8 * q, 8 * q)
        for tb in range(q // 8):
            blocks = [_gelu(yg_sc[gi, pl.ds(r0, 8), tb * LANES:(tb + 1) * LANES]) for gi in range(S5_OCT)]
            rows = _block_transpose8(blocks)
            for ti in range(0, 8, 2):
                pair = jnp.concatenate([rows[ti], rows[ti + 1]], axis=0).astype(BF16)
                y_ref[pl.ds(out0 + (tb * 8 + ti) * 8, 16), :] = pair
        return carry

    lax.fori_loop(0, n // 8, rowgroup, 0, unroll=2)


def _s5_core(ur, m, p, rt, la, lb, dt, nc):
    g, n, qh = ur.shape
    levels = (nc - 1).bit_length()
    assert nc == 1 << levels and levels <= S5_MAX_LEVELS and n % nc == 0 and qh == S5_Q * S5_GROUP
    ospec = lambda *s: pl.BlockSpec((S5_OCT,) + s, lambda i: (i,) + (0,) * len(s))
    return pl.pallas_call(
        functools.partial(_s5_core_kernel, nc=nc, levels=levels),
        grid=(g // S5_OCT,),
        in_specs=[ospec(n, qh), ospec(qh, qh), ospec(qh, LANES), ospec(qh, LANES), ospec(8, LANES),
                  ospec(8, LANES), ospec(1, qh)],
        out_specs=pl.BlockSpec((n * S5_Q, LANES), lambda i: (0, i)),
        out_shape=jax.ShapeDtypeStruct((n * S5_Q, g * S5_GROUP), BF16),
        scratch_shapes=[pltpu.VMEM((S5_OCT, n, qh), F32)],
        compiler_params=_cparams("parallel"),
        name="s5_core",
    )(ur, m, p, rt, la, lb, dt)


def _s5_out_kernel(h_ref, y_ref, z_ref, permt_ref, wg_ref, bg_ref, wo_ref, o_ref):
    blk = permt_ref.shape[0]
    y = jnp.concatenate([_dot(permt_ref[...], y_ref[r:r + blk, :]) for r in range(0, y_ref.shape[0], blk)],
                        axis=0).astype(BF16)
    g = _dot(y, wg_ref[...]) + bg_ref[...]
    y2 = y.astype(F32) * _sigmoid(g) * _silu(z_ref[...].astype(F32))
    o_ref[...] = h_ref[...] + _dot(y2.astype(BF16), wo_ref[...])


def _s5_out(h, y, z, w_glu, b_glu, w_out):
    t, d = h.shape
    d_inner = w_out.shape[0]
    blk = 8 * S5_Q
    tm = ROW_BLOCK
    assert tm % blk == 0
    return pl.pallas_call(
        _s5_out_kernel,
        grid=(t // tm,),
        in_specs=[pl.BlockSpec((tm, d), lambda i: (i, 0)), pl.BlockSpec((tm, d_inner), lambda i: (i, 0)),
                  pl.BlockSpec((tm, d_inner), lambda i: (i, 0)), _const_spec((blk, blk)), _const_spec(w_glu.shape),
                  _const_spec((1, d_inner)), _const_spec(w_out.shape)],
        out_specs=pl.BlockSpec((tm, d), lambda i: (i, 0)),
        out_shape=jax.ShapeDtypeStruct((t, d), F32),
        compiler_params=_cparams("parallel"),
        name="s5_out",
    )(h, y, z, jnp.asarray(_s5_row_perm().T, BF16), w_glu.astype(BF16), b_glu.reshape(1, d_inner),
      w_out.astype(BF16))


def _s5_layer(h, bsz, norm_g, w_in, a_re, a_im, log_step, b_re, b_im, c_re, c_im, d_skip, w_glu, b_glu, w_out):
    t, _ = h.shape
    seq = t // bsz
    g, _ = a_re.shape
    hh = b_re.shape[2]
    q = S5_Q
    nc = seq // q
    assert hh == S5_GROUP and seq % S5_ROWS == 0
    ur, z = _s5_in(h, norm_g, w_in)
    m, p, rt, la, lb = _s5_prep(a_re, a_im, log_step, b_re, b_im, c_re, c_im)
    dt = jnp.tile(d_skip.reshape(g, 1, hh), (1, 1, q))
    y = _s5_core(ur, m, p, rt, la, lb, dt, nc)
    return _s5_out(h, y, z, w_glu, b_glu, w_out)


def _rope_table_kernel(pos_ref, e_ref, cos_ref, sin_ref):
    p = pos_ref[...].astype(F32)
    ang = p[:, 0:1] * e_ref[0:1, :]
    for k in range(1, 4):
        ang = ang + p[:, k:k + 1] * e_ref[k:k + 1, :]
    cos_ref[...] = jnp.cos(ang)
    sin_ref[...] = jnp.sin(ang)


def _rope_tables(positions):
    t = positions.size
    half = MLA_ROPE // 2
    per_row = LANES // half
    inv_freq = (ROPE_THETA ** (-np.arange(0, MLA_ROPE, 2, dtype=np.float32) / np.float32(MLA_ROPE))).astype(np.float32)
    e = np.zeros((per_row, LANES), np.float32)
    for k in range(per_row):
        e[k, k * half:(k + 1) * half] = inv_freq
    rows = t // per_row
    tr = 512
    cos4, sin4 = pl.pallas_call(
        _rope_table_kernel,
        grid=(rows // tr,),
        in_specs=[pl.BlockSpec((tr, per_row), lambda i: (i, 0)), _const_spec((per_row, LANES))],
        out_specs=[pl.BlockSpec((tr, LANES), lambda i: (i, 0))] * 2,
        out_shape=[jax.ShapeDtypeStruct((rows, LANES), F32)] * 2,
        compiler_params=_cparams("parallel"),
        name="rope_table",
    )(positions.reshape(rows, per_row), jnp.asarray(e))
    return cos4.reshape(t, half), sin4.reshape(t, half)


def _rope128(x, c, s1, s2):
    return x * c - pltpu.roll(x, LANES - MLA_ROPE // 2, axis=1) * s1 + pltpu.roll(x, MLA_ROPE // 2, axis=1) * s2


def _mla_proj_kernel(h_ref, ng_ref, wa_ref, wz_ref, qg_ref, wuq_ref, kg_ref, wukv_ref, c_ref, s1_ref, s2_ref,
                     z_ref, q_ref, k_ref, v_ref, *, q_rank, kv_rank, scale):
    hn = _rms(h_ref[...], ng_ref[...]).astype(BF16)
    z_ref[...] = _dot(hn, wz_ref[...]).astype(BF16)
    a = _dot(hn, wa_ref[...])
    c, s1, s2 = c_ref[...], s1_ref[...], s2_ref[...]
    cq = _rms(a[:, 0:q_rank], qg_ref[...]).astype(BF16)
    ckv = _rms(a[:, q_rank:q_rank + kv_rank], kg_ref[...]).astype(BF16)
    kr = _rope128(a[:, q_rank + kv_rank:q_rank + kv_rank + LANES], c, s1, s2).astype(BF16)
    q = _dot(cq, wuq_ref[...]) * scale
    kv = _dot(ckv, wukv_ref[...])
    for hd in range(MLA_HEADS):
        o = hd * MLA_QK_PAD
        q_ref[:, o:o + MLA_NOPE] = q[:, o:o + MLA_NOPE].astype(BF16)
        q_ref[:, o + MLA_NOPE:o + MLA_QK_PAD] = _rope128(q[:, o + MLA_NOPE:o + MLA_QK_PAD], c, s1, s2).astype(BF16)
        k_ref[:, o:o + MLA_NOPE] = kv[:, o:o + MLA_NOPE].astype(BF16)
        k_ref[:, o + MLA_NOPE:o + MLA_QK_PAD] = kr
        v_ref[:, hd * MLA_V:(hd + 1) * MLA_V] = kv[:, o + MLA_NOPE:o + MLA_NOPE + MLA_V].astype(BF16)


def _mla_proj(h, norm_g, w_in, q_norm_g, w_uq, kv_norm_g, w_ukv, cos, sin):
    t, d = h.shape
    q_rank, kv_rank = w_uq.shape[0], w_ukv.shape[0]
    n_a = q_rank + kv_rank + MLA_ROPE
    d_inner = w_in.shape[1] - n_a
    tm = ROW_BLOCK
    wa = jnp.pad(w_in[:, :n_a], ((0, 0), (0, LANES - MLA_ROPE))).astype(BF16)
    wz = w_in[:, n_a:].astype(BF16)
    qk = MLA_NOPE + MLA_ROPE
    wuq = jnp.pad(w_uq.reshape(q_rank, MLA_HEADS, qk), ((0, 0), (0, 0), (0, MLA_QK_PAD - qk)))
    wuq = wuq.reshape(q_rank, MLA_HEADS * MLA_QK_PAD).astype(BF16)
    half = MLA_ROPE // 2
    zero = jnp.zeros_like(cos)
    ctab = jnp.concatenate([cos, cos, zero, zero], axis=1)
    s1tab = jnp.concatenate([sin, zero, zero, zero], axis=1)
    s2tab = jnp.concatenate([zero, sin, zero, zero], axis=1)
    assert ctab.shape[1] == LANES and half * 4 == LANES
    row = lambda n: pl.BlockSpec((tm, n), lambda i: (i, 0))
    return pl.pallas_call(
        functools.partial(_mla_proj_kernel, q_rank=q_rank, kv_rank=kv_rank, scale=float(qk) ** -0.5 * math.log2(math.e)),
        grid=(t // tm,),
        in_specs=[row(d), _const_spec((1, d)), _const_spec(wa.shape), _const_spec(wz.shape),
                  _const_spec((1, q_rank)), _const_spec(wuq.shape), _const_spec((1, kv_rank)),
                  _const_spec(w_ukv.shape), row(LANES), row(LANES), row(LANES)],
        out_specs=[row(d_inner), row(MLA_HEADS * MLA_QK_PAD), row(MLA_HEADS * MLA_QK_PAD), row(MLA_HEADS * MLA_V)],
        out_shape=[jax.ShapeDtypeStruct((t, d_inner), BF16),
                   jax.ShapeDtypeStruct((t, MLA_HEADS * MLA_QK_PAD), BF16),
                   jax.ShapeDtypeStruct((t, MLA_HEADS * MLA_QK_PAD), BF16),
                   jax.ShapeDtypeStruct((t, MLA_HEADS * MLA_V), BF16)],
        compiler_params=_cparams("parallel"),
        name="mla_proj",
    )(h, norm_g.reshape(1, d), wa, wz, q_norm_g.reshape(1, q_rank), wuq, kv_norm_g.reshape(1, kv_rank),
      w_ukv.astype(BF16), ctab, s1tab, s2tab)


ATT_K = 256
ATT_CQ = 256
ATT_CHAINS = 16
ATT_Q = ATT_CHAINS * ATT_CQ
ATT_LOOP_TILES = 4
ATT_QK_TILES = 2
ATT_AHEAD = 2
ATT_VROWS = MLA_V + 16


def _attn_kernel(q_ref, k_ref, v_ref, o_ref, vxt_sc, acc_sc, m_sc, *, n_q_tiles):
    qi = pl.program_id(2)
    n_kt = vxt_sc.shape[0]

    @pl.when(qi == 0)
    def _():
        row = lax.broadcasted_iota(jnp.int32, (ATT_VROWS - MLA_V, ATT_K), 0)
        ones = jnp.where(row == 0, 1.0, 0.0).astype(BF16)
        for j in range(n_kt):
            vt = v_ref[j * ATT_K:(j + 1) * ATT_K, :].astype(F32).T
            vxt_sc[j, 0:MLA_V, :] = vt.astype(BF16)
            vxt_sc[j, MLA_V:ATT_VROWS, :] = ones

    m_sc[...] = jnp.full_like(m_sc, NEG_INF)
    acc_sc[...] = jnp.zeros_like(acc_sc)

    def qk(c, j, n):
        koff = pl.multiple_of(j * ATT_K, ATT_K)
        return _dot_nt(k_ref[pl.ds(koff, n * ATT_K), :], q_ref[c * ATT_CQ:(c + 1) * ATT_CQ, :])

    def softmax(st, c, mask_off):
        qs = slice(c * ATT_CQ, (c + 1) * ATT_CQ)
        if mask_off is not None:
            kk = lax.broadcasted_iota(jnp.int32, st.shape, 0) + mask_off
            qq = lax.broadcasted_iota(jnp.int32, st.shape, 1)
            st = jnp.where(kk <= qq, st, NEG_INF)
        m_prev = m_sc[:, qs]
        m_new = jnp.maximum(m_prev, jnp.max(st, axis=0, keepdims=True))
        alpha = jnp.exp2(m_prev - m_new)
        pt = jnp.exp2(st - m_new).astype(BF16)
        m_sc[:, qs] = m_new
        return alpha, pt

    def accumulate(c, j, alpha, pt):
        qs = slice(c * ATT_CQ, (c + 1) * ATT_CQ)
        acc_sc[:, qs] = alpha * acc_sc[:, qs] + _dot(vxt_sc[j], pt)

    def run(groups):
        sts = [qk(c, j, len(masks)) for c, j, masks in groups[:ATT_AHEAD]]
        pending = None
        for gidx, (c, j, masks) in enumerate(groups):
            if gidx + ATT_AHEAD < len(groups):
                nxt = groups[gidx + ATT_AHEAD]
                sts.append(qk(nxt[0], nxt[1], len(nxt[2])))
            for i, mask_off in enumerate(masks):
                alpha, pt = softmax(sts[gidx][i * ATT_K:(i + 1) * ATT_K, :], c, mask_off)
                if pending is not None:
                    accumulate(*pending)
                pending = (c, j + i, alpha, pt)
            sts[gidx] = None
        accumulate(*pending)

    def grouped(per_chain):
        chunks = {c: [tl[i:i + ATT_QK_TILES] for i in range(0, len(tl), ATT_QK_TILES)] for c, tl in per_chain.items()}
        out = []
        for r in range(max(len(v) for v in chunks.values())):
            for c, v in chunks.items():
                if r < len(v):
                    out.append((c, v[r][0][0], [m for _, m in v[r]]))
        return out

    tiles_per_step = ATT_Q // ATT_K

    def body(i, carry):
        first = ATT_LOOP_TILES * i
        run(grouped({c: [(first + e, None) for e in range(ATT_LOOP_TILES)] for c in range(ATT_CHAINS)}))
        return carry

    base = qi * tiles_per_step
    if n_q_tiles > 1:
        lax.fori_loop(0, base // ATT_LOOP_TILES, body, 0)
    diag = {}
    for c in range(ATT_CHAINS):
        q_lo, q_hi = c * ATT_CQ, (c + 1) * ATT_CQ - 1
        tl = []
        for e in range(tiles_per_step):
            k_lo, k_hi = e * ATT_K, (e + 1) * ATT_K - 1
            if k_lo <= q_hi:
                tl.append((base + e, None if k_hi <= q_lo else k_lo - q_lo))
        diag[c] = tl
    run(grouped(diag))
    acc = acc_sc[...]
    o = acc[0:MLA_V, :] / acc[MLA_V:MLA_V + 1, :]
    o_ref[...] = o.T.astype(o_ref.dtype)


def _attention(q, k, v, bsz):
    t = q.shape[0]
    seq = t // bsz
    assert seq % ATT_Q == 0
    nq = seq // ATT_Q
    qmap = lambda b, h, i: (b * nq + i, h)
    kmap = lambda b, h, i: (b, h)
    return pl.pallas_call(
        functools.partial(_attn_kernel, n_q_tiles=nq),
        grid=(bsz, MLA_HEADS, nq),
        in_specs=[pl.BlockSpec((ATT_Q, MLA_QK_PAD), qmap), pl.BlockSpec((seq, MLA_QK_PAD), kmap),
                  pl.BlockSpec((seq, MLA_V), kmap)],
        out_specs=pl.BlockSpec((ATT_Q, MLA_V), qmap),
        out_shape=jax.ShapeDtypeStruct((t, MLA_HEADS * MLA_V), BF16),
        scratch_shapes=[pltpu.VMEM((seq // ATT_K, ATT_VROWS, ATT_K), BF16), pltpu.VMEM((ATT_VROWS, ATT_Q), F32),
                        pltpu.VMEM((1, ATT_Q), F32)],
        compiler_params=_cparams("parallel", "parallel", "arbitrary"),
        name="mla_attn",
    )(q, k, v)


def _mla_layer(h, bsz, positions, norm_g, w_in, q_norm_g, w_uq, kv_norm_g, w_ukv):
    cos, sin = _rope_tables(positions)
    z, q, k, v = _mla_proj(h, norm_g, w_in, q_norm_g, w_uq, kv_norm_g, w_ukv, cos, sin)
    return _attention(q, k, v, bsz), z


def kernel(x, positions, l0_norm_g, l0_w_in, l0_ln_g, l0_ln_b, l0_w_s, l0_b_s, l0_w_out, l1_norm_g, l1_w_in, l1_a_re, l1_a_im, l1_log_step, l1_b_re, l1_b_im, l1_c_re, l1_c_im, l1_d_skip, l1_w_glu, l1_b_glu, l1_w_out, l2_norm_g, l2_w_in, l2_q_norm_g, l2_w_uq, l2_kv_norm_g, l2_w_ukv, l2_w_out, l3_norm_g, l3_w_in, l3_ln_g, l3_ln_b, l3_w_s, l3_b_s, l3_w_out, final_norm_g):
    bsz, seq, d = x.shape
    h = x.reshape(bsz * seq, d)
    h = _gmlp_layer(h, l0_norm_g, l0_w_in, l0_ln_g, l0_ln_b, l0_w_s, l0_b_s, l0_w_out)
    h = _s5_layer(h, bsz, l1_norm_g, l1_w_in, l1_a_re, l1_a_im, l1_log_step, l1_b_re, l1_b_im, l1_c_re, l1_c_im,
                  l1_d_skip, l1_w_glu, l1_b_glu, l1_w_out)
    o, z = _mla_layer(h, bsz, positions, l2_norm_g, l2_w_in, l2_q_norm_g, l2_w_uq, l2_kv_norm_g, l2_w_ukv)
    h = _gmlp_layer(h, l3_norm_g, l3_w_in, l3_ln_g, l3_ln_b, l3_w_s, l3_b_s, l3_w_out, final_g=final_norm_g,
                    pre=(o, z, l2_w_out))
    return h.reshape(bsz, seq, d)
```

```python
import functools
import math

import jax
import jax.numpy as jnp
import numpy as np
from jax import lax
from jax.experimental import pallas as pl
from jax.experimental.pallas import tpu as pltpu

F32 = jnp.float32
BF16 = jnp.bfloat16

NORM_EPS = 1e-6
LANES = 128
VMEM_LIMIT = 56 * 1024 * 1024

GMLP_CHUNK = 128
GMLP_GROUPS = 8
S5_GROUP = 16
S5_STATE = 64
S5_Q = 32
MLA_HEADS = 16
MLA_NOPE = 128
MLA_ROPE = 64
MLA_V = 128
MLA_QK_PAD = 256
ROPE_THETA = 10000.0
NEG_INF = -1e30

ROW_BLOCK = 512


def _cparams(*sem):
    return pltpu.CompilerParams(dimension_semantics=sem, vmem_limit_bytes=VMEM_LIMIT)


def _const_spec(shape):
    nd = len(shape)
    return pl.BlockSpec(shape, lambda *_: (0,) * nd, pipeline_mode=pl.Buffered(1))


def _dot(a, b):
    return jnp.dot(a, b, preferred_element_type=F32)


def _dot_nt(a, b, precision=None):
    return lax.dot_general(a, b, (((1,), (1,)), ((), ())), preferred_element_type=F32, precision=precision)


def _rms(x, g):
    return x * lax.rsqrt(jnp.mean(x * x, axis=-1, keepdims=True) + NORM_EPS) * g


def _gelu(x):
    return 0.5 * x * (1.0 + jnp.tanh(math.sqrt(2.0 / math.pi) * (x + 0.044715 * (x * x * x))))


def _sigmoid(x):
    return 0.5 * (1.0 + jnp.tanh(0.5 * x))


def _silu(x):
    return x * _sigmoid(x)


def _gmlp_kernel(h_ref, ng_ref, win_ref, lng_ref, lnb_ref, ws_ref, bst_ref, wout_ref, *rest, d_inner, final, fused_in):
    rest = list(rest)
    o_ref = rest.pop()
    fg_ref = rest.pop() if final else None
    x = h_ref[...]
    if fused_in:
        po_ref, pz_ref, pw_ref = rest
        x = x + _dot((po_ref[...].astype(F32) * _silu(pz_ref[...].astype(F32))).astype(BF16), pw_ref[...])
    tm = x.shape[0]
    hn = _rms(x, ng_ref[...]).astype(BF16)
    u = _gelu(_dot(hn, win_ref[:, 0:d_inner]))
    v = _gelu(_dot(hn, win_ref[:, d_inner:2 * d_inner]))
    mu = jnp.mean(v, axis=-1, keepdims=True)
    vc = v - mu
    var = jnp.mean(vc * vc, axis=-1, keepdims=True)
    vn = (vc * lax.rsqrt(var + NORM_EPS) * lng_ref[...] + lnb_ref[...]).astype(BF16)
    z = _dot(hn, win_ref[:, 2 * d_inner:3 * d_inner])
    gate = u * _silu(z)

    gw = d_inner // GMLP_GROUPS
    r = lax.broadcasted_iota(jnp.int32, (GMLP_CHUNK, GMLP_CHUNK), 0)
    c = lax.broadcasted_iota(jnp.int32, (GMLP_CHUNK, GMLP_CHUNK), 1)
    causal = c <= r
    cols = []
    for g in range(GMLP_GROUPS):
        w = jnp.where(causal, ws_ref[g], 0.0).astype(BF16)
        bias = bst_ref[:, g:g + 1]
        rows = []
        for ci in range(tm // GMLP_CHUNK):
            vs = vn[ci * GMLP_CHUNK:(ci + 1) * GMLP_CHUNK, g * gw:(g + 1) * gw]
            rows.append(_dot(w, vs) + bias)
        cols.append(jnp.concatenate(rows, axis=0) if len(rows) > 1 else rows[0])
    s = jnp.concatenate(cols, axis=1)
    y = (gate * s).astype(BF16)
    out = x + _dot(y, wout_ref[...])
    if final:
        out = _rms(out, fg_ref[...])
    o_ref[...] = out


def _gmlp_layer(h, norm_g, w_in, ln_g, ln_b, w_s, b_s, w_out, final_g=None, pre=None):
    t, d = h.shape
    d_inner = w_out.shape[0]
    tm = ROW_BLOCK
    final = final_g is not None
    args = [h, norm_g.reshape(1, d), w_in.astype(BF16), ln_g.reshape(1, d_inner), ln_b.reshape(1, d_inner),
            w_s, b_s.T, w_out.astype(BF16)]
    specs = [pl.BlockSpec((tm, d), lambda i: (i, 0)), _const_spec((1, d)), _const_spec(w_in.shape),
             _const_spec((1, d_inner)), _const_spec((1, d_inner)), _const_spec(w_s.shape),
             _const_spec((GMLP_CHUNK, GMLP_GROUPS)), _const_spec(w_out.shape)]
    if pre is not None:
        po, pz, pw = pre
        args += [po, pz, pw.astype(BF16)]
        specs += [pl.BlockSpec((tm, po.shape[1]), lambda i: (i, 0)), pl.BlockSpec((tm, pz.shape[1]), lambda i: (i, 0)),
                  _const_spec(pw.shape)]
    if final:
        args.append(final_g.reshape(1, d))
        specs.append(_const_spec((1, d)))
    return pl.pallas_call(
        functools.partial(_gmlp_kernel, d_inner=d_inner, final=final, fused_in=pre is not None),
        grid=(t // tm,),
        in_specs=specs,
        out_specs=pl.BlockSpec((tm, d), lambda i: (i, 0)),
        out_shape=jax.ShapeDtypeStruct((t, d), F32),
        compiler_params=_cparams("parallel"),
        name="gmlp_final" if final else "gmlp",
    )(*args)


S5_OCT = LANES // S5_GROUP
S5_ROWS = 512


def _s5_lane_perm():
    n = S5_OCT * LANES
    p = np.zeros((n, n), np.float32)
    for i in range(S5_OCT):
        for j in range(S5_OCT):
            for ch in range(S5_GROUP):
                p[i * LANES + j * S5_GROUP + ch, j * LANES + i * S5_GROUP + ch] = 1.0
    return p


def _block_transpose8(v, lperm):
    t = _dot(jnp.concatenate(v, axis=1), lperm)
    return [t[:, j * LANES:(j + 1) * LANES].astype(BF16) for j in range(S5_OCT)]


def _s5_row_perm(chunks=8):
    n = chunks * S5_Q
    p = np.zeros((n, n), np.float32)
    for c in range(chunks):
        for s in range(S5_Q):
            p[s * chunks + c, c * S5_Q + s] = 1.0
    return p


def _s5_in_kernel(h_ref, ng_ref, win_ref, perm_ref, lperm_ref, ur_ref, z_ref, *, d_inner):
    hn = _rms(h_ref[...], ng_ref[...]).astype(BF16)
    z_ref[...] = _dot(hn, win_ref[:, d_inner:2 * d_inner]).astype(BF16)
    u = _dot(hn, win_ref[:, 0:d_inner]).astype(BF16)
    n_oct = d_inner // LANES
    q = S5_Q
    blk_rows = 8 * q
    assert S5_ROWS == 2 * blk_rows
    ups = [_dot(perm_ref[...], u[cb * blk_rows:(cb + 1) * blk_rows, :]).reshape(q, 8, d_inner) for cb in range(2)]
    tiled = jnp.concatenate(ups, axis=1).astype(BF16)
    xs = []
    for si in range(8):
        xs.append(jnp.concatenate([tiled[sb * 8 + si, :, o * LANES:(o + 1) * LANES]
                                   for o in range(n_oct) for sb in range(q // 8)], axis=0))
    ts = _block_transpose8(xs, lperm_ref[...])
    for o in range(n_oct):
        for gi in range(S5_OCT):
            tiles = [ts[gi][(o * (q // 8) + sb) * 16:(o * (q // 8) + sb + 1) * 16, :] for sb in range(q // 8)]
            ur_ref[o * S5_OCT + gi] = jnp.concatenate(tiles, axis=1)


def _s5_in(h, norm_g, w_in):
    t, d = h.shape
    d_inner = w_in.shape[1] // 2
    g = d_inner // S5_GROUP
    tm = S5_ROWS
    return pl.pallas_call(
        functools.partial(_s5_in_kernel, d_inner=d_inner),
        grid=(t // tm,),
        in_specs=[pl.BlockSpec((tm, d), lambda i: (i, 0)), _const_spec((1, d)), _const_spec(w_in.shape),
                  _const_spec((8 * S5_Q, 8 * S5_Q)), _const_spec((S5_OCT * LANES, S5_OCT * LANES))],
        out_specs=[pl.BlockSpec((g, tm // S5_Q, S5_Q * S5_GROUP), lambda i: (0, i, 0)),
                   pl.BlockSpec((tm, d_inner), lambda i: (i, 0))],
        out_shape=[jax.ShapeDtypeStruct((g, t // S5_Q, S5_Q * S5_GROUP), BF16),
                   jax.ShapeDtypeStruct((t, d_inner), BF16)],
        compiler_params=_cparams("parallel"),
        name="s5_in",
    )(h, norm_g.reshape(1, d), w_in.astype(BF16), jnp.asarray(_s5_row_perm(), BF16),
      jnp.asarray(_s5_lane_perm(), BF16))


S5_MAX_LEVELS = 8
S5_PREP_GROUPS = 8
S5_POW_ROWS = 40


def _s5_prep_kernel(are_ref, aim_ref, ls_ref, bx_ref, by_ref, cr_ref, ci_ref,
                    m_ref, p_ref, rt_ref, la_ref, lb_ref, a_sc):
    q, hh = S5_Q, S5_GROUP
    lane = lax.broadcasted_iota(jnp.int32, (1, LANES), 1)
    lo = lane < S5_STATE
    sgn = jnp.where(lo, -1.0, 1.0).astype(F32)

    j = lax.broadcasted_iota(jnp.int32, (S5_POW_ROWS, LANES), 0).astype(F32)

    def group(gi):
        step = jnp.exp(ls_ref[gi])
        ar, ai = are_ref[gi], aim_ref[gi]
        lr, li = ar * step, ai * step
        mag = jnp.exp(j * lr)
        er, ei = mag * jnp.cos(j * li), mag * jnp.sin(j * li)
        nr, ni = er[1:2] - 1.0, ei[1:2]
        den = 1.0 / (ar * ar + ai * ai)
        cfr = (nr * ar + ni * ai) * den
        cfi = (ni * ar - nr * ai) * den

        bx, by = bx_ref[gi], by_ref[gi]
        bba = cfr * bx + cfi * sgn * by
        bbb = sgn * cfr * by - cfi * bx
        bbt = -(sgn * cfr * bx) - cfi * by

        c1 = cr_ref[gi]
        c2 = sgn * ci_ref[gi]
        ee1 = jnp.where(lo, er, ei)
        ee2 = jnp.where(lo, ei, er)
        for jj in range(q + 1):
            a_sc[gi, jj * hh:(jj + 1) * hh, :] = c1 * ee1[jj:jj + 1] + c2 * ee2[jj:jj + 1]

        for s in range(q):
            e = q - 1 - s
            p_ref[gi, s * hh:(s + 1) * hh, :] = (er[e:e + 1] * bba + ei[e:e + 1] * bbb).astype(BF16)

        rt_ref[gi] = (-sgn * a_sc[gi, hh:(q + 1) * hh, :]).astype(BF16)

        kt = _dot_nt(bbt, a_sc[gi, 0:q * hh, :], precision=lax.Precision.HIGHEST)
        col = lax.broadcasted_iota(jnp.int32, kt.shape, 1)
        for s in range(q):
            if s == 0:
                blk = kt
            else:
                blk = jnp.where(col >= s * hh, pltpu.roll(kt, s * hh, axis=1), 0.0)
            m_ref[gi, s * hh:(s + 1) * hh, :] = blk.astype(BF16)

        pr, pi = er[q:q + 1], ei[q:q + 1]
        for k in range(S5_MAX_LEVELS):
            la_ref[gi, k:k + 1, :] = pr
            lb_ref[gi, k:k + 1, :] = sgn * pi
            pr, pi = pr * pr - pi * pi, 2.0 * pr * pi

    for gi in range(are_ref.shape[0]):
        group(gi)


def _s5_prep(a_re, a_im, log_step, b_re, b_im, c_re, c_im):
    g, p = a_re.shape
    hh = b_re.shape[2]
    qh = S5_Q * hh
    dup = lambda x: jnp.concatenate([x, x], axis=-1)
    are2 = dup(a_re).reshape(g, 1, 2 * p)
    aim2 = dup(a_im).reshape(g, 1, 2 * p)
    ls = log_step.reshape(g, 1, 1)
    brt, bit = jnp.swapaxes(b_re, 1, 2), jnp.swapaxes(b_im, 1, 2)
    bx = jnp.concatenate([brt, bit], axis=-1)
    by = jnp.concatenate([bit, brt], axis=-1)
    gb = S5_PREP_GROUPS
    gspec = lambda *s: pl.BlockSpec((gb,) + s, lambda i: (i,) + (0,) * len(s))
    return pl.pallas_call(
        _s5_prep_kernel,
        grid=(g // gb,),
        in_specs=[gspec(1, 2 * p), gspec(1, 2 * p), gspec(1, 1), gspec(hh, 2 * p), gspec(hh, 2 * p),
                  gspec(hh, 2 * p), gspec(hh, 2 * p)],
        out_specs=[gspec(qh, qh), gspec(qh, 2 * p), gspec(qh, 2 * p), gspec(8, 2 * p), gspec(8, 2 * p)],
        out_shape=[jax.ShapeDtypeStruct((g, qh, qh), BF16), jax.ShapeDtypeStruct((g, qh, 2 * p), BF16),
                   jax.ShapeDtypeStruct((g, qh, 2 * p), BF16), jax.ShapeDtypeStruct((g, 8, 2 * p), F32),
                   jax.ShapeDtypeStruct((g, 8, 2 * p), F32)],
        scratch_shapes=[pltpu.VMEM((gb, (S5_Q + 1) * hh, 2 * p), F32)],
        compiler_params=_cparams("parallel"),
        name="s5_prep",
    )(are2, aim2, ls, bx, by, dup(c_re), dup(c_im))


def _s5_core_kernel(u_ref, m_ref, p_ref, rt_ref, la_ref, lb_ref, d_ref, lperm_ref, y_ref, x_sc, *, nc, levels):
    n = u_ref.shape[1]
    q = S5_Q
    tiles = n // 16
    cidx = lax.broadcasted_iota(jnp.int32, (n, LANES), 0) & (nc - 1)
    for gi in range(S5_OCT):
        u = u_ref[gi]
        z = _dot(u, p_ref[gi])
        la, lb = la_ref[gi], lb_ref[gi]
        for k in range(levels):
            sh = 1 << k
            zs = jnp.where(cidx >= sh, pltpu.roll(z, sh, axis=0), 0.0)
            zw = pltpu.roll(zs, S5_STATE, axis=1)
            z = z + la[k:k + 1] * zs + lb[k:k + 1] * zw
        x = jnp.where(cidx >= 1, pltpu.roll(z, 1, axis=0), 0.0)
        y = _dot(u, m_ref[gi]) + _dot_nt(x.astype(BF16), rt_ref[gi]) + u.astype(F32) * d_ref[gi]
        yb = _gelu(y).astype(BF16)
        for tb in range(q // 8):
            x_sc[tb * n:(tb + 1) * n, gi * LANES:(gi + 1) * LANES] = yb[:, tb * LANES:(tb + 1) * LANES]

    for tb in range(q // 8):
        t = _dot(x_sc[tb * n:(tb + 1) * n, :], lperm_ref[...])
        for ti in range(8):
            blk = t[:, ti * LANES:(ti + 1) * LANES].astype(BF16)
            y_ref[:, tb * 8 + ti, :, :] = blk.reshape(tiles, 16, LANES)


def _s5_core(ur, m, p, rt, la, lb, dt, nc):
    g, n, qh = ur.shape
    levels = (nc - 1).bit_length()
    assert nc == 1 << levels and levels <= S5_MAX_LEVELS and n % nc == 0 and qh == S5_Q * S5_GROUP and n % 16 == 0
    ospec = lambda *s: pl.BlockSpec((S5_OCT,) + s, lambda i: (i,) + (0,) * len(s))
    return pl.pallas_call(
        functools.partial(_s5_core_kernel, nc=nc, levels=levels),
        grid=(g // S5_OCT,),
        in_specs=[ospec(n, qh), ospec(qh, qh), ospec(qh, LANES), ospec(qh, LANES), ospec(8, LANES),
                  ospec(8, LANES), ospec(1, qh), _const_spec((S5_OCT * LANES, S5_OCT * LANES))],
        out_specs=pl.BlockSpec((n // 16, S5_Q, 16, LANES), lambda i: (0, 0, 0, i)),
        out_shape=jax.ShapeDtypeStruct((n // 16, S5_Q, 16, g * S5_GROUP), BF16),
        scratch_shapes=[pltpu.VMEM((n * (qh // LANES), S5_OCT * LANES), BF16)],
        compiler_params=_cparams("parallel"),
        name="s5_core",
    )(ur, m, p, rt, la, lb, dt, jnp.asarray(_s5_lane_perm(), BF16))


def _s5_out_kernel(h_ref, y_ref, z_ref, permt_ref, wg_ref, bg_ref, wo_ref, o_ref):
    blk = permt_ref.shape[0]
    y = jnp.concatenate([_dot(permt_ref[...], y_ref[r:r + blk, :]) for r in range(0, y_ref.shape[0], blk)],
                        axis=0).astype(BF16)
    g = _dot(y, wg_ref[...]) + bg_ref[...]
    y2 = y.astype(F32) * _sigmoid(g) * _silu(z_ref[...].astype(F32))
    o_ref[...] = h_ref[...] + _dot(y2.astype(BF16), wo_ref[...])


def _s5_out(h, y, z, w_glu, b_glu, w_out):
    t, d = h.shape
    d_inner = w_out.shape[0]
    blk = 16 * S5_Q
    tm = ROW_BLOCK
    assert tm % blk == 0
    return pl.pallas_call(
        _s5_out_kernel,
        grid=(t // tm,),
        in_specs=[pl.BlockSpec((tm, d), lambda i: (i, 0)), pl.BlockSpec((tm, d_inner), lambda i: (i, 0)),
                  pl.BlockSpec((tm, d_inner), lambda i: (i, 0)), _const_spec((blk, blk)), _const_spec(w_glu.shape),
                  _const_spec((1, d_inner)), _const_spec(w_out.shape)],
        out_specs=pl.BlockSpec((tm, d), lambda i: (i, 0)),
        out_shape=jax.ShapeDtypeStruct((t, d), F32),
        compiler_params=_cparams("parallel"),
        name="s5_out",
    )(h, y, z, jnp.asarray(_s5_row_perm(16).T, BF16), w_glu.astype(BF16), b_glu.reshape(1, d_inner),
      w_out.astype(BF16))


def _s5_layer(h, bsz, norm_g, w_in, a_re, a_im, log_step, b_re, b_im, c_re, c_im, d_skip, w_glu, b_glu, w_out):
    t, _ = h.shape
    seq = t // bsz
    g, _ = a_re.shape
    hh = b_re.shape[2]
    q = S5_Q
    nc = seq // q
    assert hh == S5_GROUP and seq % S5_ROWS == 0
    ur, z = _s5_in(h, norm_g, w_in)
    m, p, rt, la, lb = _s5_prep(a_re, a_im, log_step, b_re, b_im, c_re, c_im)
    dt = jnp.tile(d_skip.reshape(g, 1, hh), (1, 1, q))
    y = _s5_core(ur, m, p, rt, la, lb, dt, nc)
    return _s5_out(h, y.reshape(t, g * hh), z, w_glu, b_glu, w_out)


def _rope_table_kernel(pos_ref, e_ref, cos_ref, sin_ref):
    p = pos_ref[...].astype(F32)
    ang = p[:, 0:1] * e_ref[0:1, :]
    for k in range(1, 4):
        ang = ang + p[:, k:k + 1] * e_ref[k:k + 1, :]
    cos_ref[...] = jnp.cos(ang)
    sin_ref[...] = jnp.sin(ang)


def _rope_tables(positions):
    t = positions.size
    half = MLA_ROPE // 2
    per_row = LANES // half
    inv_freq = (ROPE_THETA ** (-np.arange(0, MLA_ROPE, 2, dtype=np.float32) / np.float32(MLA_ROPE))).astype(np.float32)
    e = np.zeros((per_row, LANES), np.float32)
    for k in range(per_row):
        e[k, k * half:(k + 1) * half] = inv_freq
    rows = t // per_row
    tr = 512
    cos4, sin4 = pl.pallas_call(
        _rope_table_kernel,
        grid=(rows // tr,),
        in_specs=[pl.BlockSpec((tr, per_row), lambda i: (i, 0)), _const_spec((per_row, LANES))],
        out_specs=[pl.BlockSpec((tr, LANES), lambda i: (i, 0))] * 2,
        out_shape=[jax.ShapeDtypeStruct((rows, LANES), F32)] * 2,
        compiler_params=_cparams("parallel"),
        name="rope_table",
    )(positions.reshape(rows, per_row), jnp.asarray(e))
    return cos4.reshape(t, half), sin4.reshape(t, half)


def _rope128(x, c, s1, s2):
    return x * c - pltpu.roll(x, LANES - MLA_ROPE // 2, axis=1) * s1 + pltpu.roll(x, MLA_ROPE // 2, axis=1) * s2


def _mla_proj_kernel(h_ref, ng_ref, wa_ref, wz_ref, qg_ref, wuq_ref, kg_ref, wukv_ref, c_ref, s1_ref, s2_ref,
                     z_ref, q_ref, k_ref, v_ref, *, q_rank, kv_rank, scale):
    hn = _rms(h_ref[...], ng_ref[...]).astype(BF16)
    z_ref[...] = _dot(hn, wz_ref[...]).astype(BF16)
    a = _dot(hn, wa_ref[...])
    c, s1, s2 = c_ref[...], s1_ref[...], s2_ref[...]
    cq = _rms(a[:, 0:q_rank], qg_ref[...]).astype(BF16)
    ckv = _rms(a[:, q_rank:q_rank + kv_rank], kg_ref[...]).astype(BF16)
    kr = _rope128(a[:, q_rank + kv_rank:q_rank + kv_rank + LANES], c, s1, s2).astype(BF16)
    q = _dot(cq, wuq_ref[...]) * scale
    kv = _dot(ckv, wukv_ref[...])
    for hd in range(MLA_HEADS):
        o = hd * MLA_QK_PAD
        q_ref[:, o:o + MLA_NOPE] = q[:, o:o + MLA_NOPE].astype(BF16)
        q_ref[:, o + MLA_NOPE:o + MLA_QK_PAD] = _rope128(q[:, o + MLA_NOPE:o + MLA_QK_PAD], c, s1, s2).astype(BF16)
        k_ref[:, o:o + MLA_NOPE] = kv[:, o:o + MLA_NOPE].astype(BF16)
        k_ref[:, o + MLA_NOPE:o + MLA_QK_PAD] = kr
        v_ref[:, hd * MLA_V:(hd + 1) * MLA_V] = kv[:, o + MLA_NOPE:o + MLA_NOPE + MLA_V].astype(BF16)


def _mla_proj(h, norm_g, w_in, q_norm_g, w_uq, kv_norm_g, w_ukv, cos, sin):
    t, d = h.shape
    q_rank, kv_rank = w_uq.shape[0], w_ukv.shape[0]
    n_a = q_rank + kv_rank + MLA_ROPE
    d_inner = w_in.shape[1] - n_a
    tm = ROW_BLOCK
    wa = jnp.pad(w_in[:, :n_a], ((0, 0), (0, LANES - MLA_ROPE))).astype(BF16)
    wz = w_in[:, n_a:].astype(BF16)
    qk = MLA_NOPE + MLA_ROPE
    wuq = jnp.pad(w_uq.reshape(q_rank, MLA_HEADS, qk), ((0, 0), (0, 0), (0, MLA_QK_PAD - qk)))
    wuq = wuq.reshape(q_rank, MLA_HEADS * MLA_QK_PAD).astype(BF16)
    half = MLA_ROPE // 2
    zero = jnp.zeros_like(cos)
    ctab = jnp.concatenate([cos, cos, zero, zero], axis=1)
    s1tab = jnp.concatenate([sin, zero, zero, zero], axis=1)
    s2tab = jnp.concatenate([zero, sin, zero, zero], axis=1)
    assert ctab.shape[1] == LANES and half * 4 == LANES
    row = lambda n: pl.BlockSpec((tm, n), lambda i: (i, 0))
    return pl.pallas_call(
        functools.partial(_mla_proj_kernel, q_rank=q_rank, kv_rank=kv_rank, scale=float(qk) ** -0.5 * math.log2(math.e)),
        grid=(t // tm,),
        in_specs=[row(d), _const_spec((1, d)), _const_spec(wa.shape), _const_spec(wz.shape),
                  _const_spec((1, q_rank)), _const_spec(wuq.shape), _const_spec((1, kv_rank)),
                  _const_spec(w_ukv.shape), row(LANES), row(LANES), row(LANES)],
        out_specs=[row(d_inner), row(MLA_HEADS * MLA_QK_PAD), row(MLA_HEADS * MLA_QK_PAD), row(MLA_HEADS * MLA_V)],
        out_shape=[jax.ShapeDtypeStruct((t, d_inner), BF16),
                   jax.ShapeDtypeStruct((t, MLA_HEADS * MLA_QK_PAD), BF16),
                   jax.ShapeDtypeStruct((t, MLA_HEADS * MLA_QK_PAD), BF16),
                   jax.ShapeDtypeStruct((t, MLA_HEADS * MLA_V), BF16)],
        compiler_params=_cparams("parallel"),
        name="mla_proj",
    )(h, norm_g.reshape(1, d), wa, wz, q_norm_g.reshape(1, q_rank), wuq, kv_norm_g.reshape(1, kv_rank),
      w_ukv.astype(BF16), ctab, s1tab, s2tab)


ATT_K = 256
ATT_CQ = 256
ATT_CHAINS = 16
ATT_Q = ATT_CHAINS * ATT_CQ
ATT_LOOP_TILES = 4
ATT_QK_TILES = 2
ATT_AHEAD = 2
ATT_VROWS = MLA_V + 16


def _attn_kernel(q_ref, k_ref, v_ref, o_ref, vxt_sc, acc_sc, m_sc, *, n_q_tiles):
    qi = pl.program_id(2)
    n_kt = vxt_sc.shape[0]

    @pl.when(qi == 0)
    def _():
        row = lax.broadcasted_iota(jnp.int32, (ATT_VROWS - MLA_V, ATT_K), 0)
        ones = jnp.where(row == 0, 1.0, 0.0).astype(BF16)
        for j in range(n_kt):
            vt = v_ref[j * ATT_K:(j + 1) * ATT_K, :].astype(F32).T
            vxt_sc[j, 0:MLA_V, :] = vt.astype(BF16)
            vxt_sc[j, MLA_V:ATT_VROWS, :] = ones

    m_sc[...] = jnp.full_like(m_sc, NEG_INF)
    acc_sc[...] = jnp.zeros_like(acc_sc)

    def qk(c, j, n):
        koff = pl.multiple_of(j * ATT_K, ATT_K)
        return _dot_nt(k_ref[pl.ds(koff, n * ATT_K), :], q_ref[c * ATT_CQ:(c + 1) * ATT_CQ, :])

    def softmax(st, c, mask_off):
        qs = slice(c * ATT_CQ, (c + 1) * ATT_CQ)
        if mask_off is not None:
            kk = lax.broadcasted_iota(jnp.int32, st.shape, 0) + mask_off
            qq = lax.broadcasted_iota(jnp.int32, st.shape, 1)
            st = jnp.where(kk <= qq, st, NEG_INF)
        m_prev = m_sc[:, qs]
        m_new = jnp.maximum(m_prev, jnp.max(st, axis=0, keepdims=True))
        alpha = jnp.exp2(m_prev - m_new)
        pt = jnp.exp2(st - m_new).astype(BF16)
        m_sc[:, qs] = m_new
        return alpha, pt

    def accumulate(c, j, alpha, pt):
        qs = slice(c * ATT_CQ, (c + 1) * ATT_CQ)
        acc_sc[:, qs] = alpha * acc_sc[:, qs] + _dot(vxt_sc[j], pt)

    def run(groups):
        sts = [qk(c, j, len(masks)) for c, j, masks in groups[:ATT_AHEAD]]
        pending = None
        for gidx, (c, j, masks) in enumerate(groups):
            if gidx + ATT_AHEAD < len(groups):
                nxt = groups[gidx + ATT_AHEAD]
                sts.append(qk(nxt[0], nxt[1], len(nxt[2])))
            for i, mask_off in enumerate(masks):
                alpha, pt = softmax(sts[gidx][i * ATT_K:(i + 1) * ATT_K, :], c, mask_off)
                if pending is not None:
                    accumulate(*pending)
                pending = (c, j + i, alpha, pt)
            sts[gidx] = None
        accumulate(*pending)

    def grouped(per_chain):
        chunks = {c: [tl[i:i + ATT_QK_TILES] for i in range(0, len(tl), ATT_QK_TILES)] for c, tl in per_chain.items()}
        out = []
        for r in range(max(len(v) for v in chunks.values())):
            for c, v in chunks.items():
                if r < len(v):
                    out.append((c, v[r][0][0], [m for _, m in v[r]]))
        return out

    tiles_per_step = ATT_Q // ATT_K

    def body(i, carry):
        first = ATT_LOOP_TILES * i
        run(grouped({c: [(first + e, None) for e in range(ATT_LOOP_TILES)] for c in range(ATT_CHAINS)}))
        return carry

    base = qi * tiles_per_step
    if n_q_tiles > 1:
        lax.fori_loop(0, base // ATT_LOOP_TILES, body, 0)
    diag = {}
    for c in range(ATT_CHAINS):
        q_lo, q_hi = c * ATT_CQ, (c + 1) * ATT_CQ - 1
        tl = []
        for e in range(tiles_per_step):
            k_lo, k_hi = e * ATT_K, (e + 1) * ATT_K - 1
            if k_lo <= q_hi:
                tl.append((base + e, None if k_hi <= q_lo else k_lo - q_lo))
        diag[c] = tl
    run(grouped(diag))
    acc = acc_sc[...]
    o = acc[0:MLA_V, :] / acc[MLA_V:MLA_V + 1, :]
    o_ref[...] = o.T.astype(o_ref.dtype)


def _attention(q, k, v, bsz):
    t = q.shape[0]
    seq = t // bsz
    assert seq % ATT_Q == 0
    nq = seq // ATT_Q
    qmap = lambda b, h, i: (b * nq + i, h)
    kmap = lambda b, h, i: (b, h)
    return pl.pallas_call(
        functools.partial(_attn_kernel, n_q_tiles=nq),
        grid=(bsz, MLA_HEADS, nq),
        in_specs=[pl.BlockSpec((ATT_Q, MLA_QK_PAD), qmap), pl.BlockSpec((seq, MLA_QK_PAD), kmap),
                  pl.BlockSpec((seq, MLA_V), kmap)],
        out_specs=pl.BlockSpec((ATT_Q, MLA_V), qmap),
        out_shape=jax.ShapeDtypeStruct((t, MLA_HEADS * MLA_V), BF16),
        scratch_shapes=[pltpu.VMEM((seq // ATT_K, ATT_VROWS, ATT_K), BF16), pltpu.VMEM((ATT_VROWS, ATT_Q), F32),
                        pltpu.VMEM((1, ATT_Q), F32)],
        compiler_params=_cparams("parallel", "parallel", "arbitrary"),
        name="mla_attn",
    )(q, k, v)


def _mla_layer(h, bsz, positions, norm_g, w_in, q_norm_g, w_uq, kv_norm_g, w_ukv):
    cos, sin = _rope_tables(positions)
    z, q, k, v = _mla_proj(h, norm_g, w_in, q_norm_g, w_uq, kv_norm_g, w_ukv, cos, sin)
    return _attention(q, k, v, bsz), z


def kernel(x, positions, l0_norm_g, l0_w_in, l0_ln_g, l0_ln_b, l0_w_s, l0_b_s, l0_w_out, l1_norm_g, l1_w_in, l1_a_re, l1_a_im, l1_log_step, l1_b_re, l1_b_im, l1_c_re, l1_c_im, l1_d_skip, l1_w_glu, l1_b_glu, l1_w_out, l2_norm_g, l2_w_in, l2_q_norm_g, l2_w_uq, l2_kv_norm_g, l2_w_ukv, l2_w_out, l3_norm_g, l3_w_in, l3_ln_g, l3_ln_b, l3_w_s, l3_b_s, l3_w_out, final_norm_g):
    bsz, seq, d = x.shape
    h = x.reshape(bsz * seq, d)
    h = _gmlp_layer(h, l0_norm_g, l0_w_in, l0_ln_g, l0_ln_b, l0_w_s, l0_b_s, l0_w_out)
    h = _s5_layer(h, bsz, l1_norm_g, l1_w_in, l1_a_re, l1_a_im, l1_log_step, l1_b_re, l1_b_im, l1_c_re, l1_c_im,
                  l1_d_skip, l1_w_glu, l1_b_glu, l1_w_out)
    o, z = _mla_layer(h, bsz, positions, l2_norm_g, l2_w_in, l2_q_norm_g, l2_w_uq, l2_kv_norm_g, l2_w_ukv)
    h = _gmlp_layer(h, l3_norm_g, l3_w_in, l3_ln_g, l3_ln_b, l3_w_s, l3_b_s, l3_w_out, final_g=final_norm_g,
                    pre=(o, z, l2_w_out))
    return h.reshape(bsz, seq, d)
```

```python
import functools
import math

import jax
import jax.numpy as jnp
import numpy as np
from jax import lax
from jax.experimental import pallas as pl
from jax.experimental.pallas import tpu as pltpu

F32 = jnp.float32
BF16 = jnp.bfloat16

NORM_EPS = 1e-6
LANES = 128
VMEM_LIMIT = 56 * 1024 * 1024

GMLP_CHUNK = 128
GMLP_GROUPS = 8
S5_GROUP = 16
S5_STATE = 64
S5_Q = 32
MLA_HEADS = 16
MLA_NOPE = 128
MLA_ROPE = 64
MLA_V = 128
MLA_QK_PAD = 256
ROPE_THETA = 10000.0
NEG_INF = -1e30

ROW_BLOCK = 512


def _cparams(*sem):
    return pltpu.CompilerParams(dimension_semantics=sem, vmem_limit_bytes=VMEM_LIMIT)


def _const_spec(shape):
    nd = len(shape)
    return pl.BlockSpec(shape, lambda *_: (0,) * nd, pipeline_mode=pl.Buffered(1))


def _dot(a, b):
    return jnp.dot(a, b, preferred_element_type=F32)


def _dot_nt(a, b, precision=None):
    return lax.dot_general(a, b, (((1,), (1,)), ((), ())), preferred_element_type=F32, precision=precision)


def _rms(x, g):
    return x * lax.rsqrt(jnp.mean(x * x, axis=-1, keepdims=True) + NORM_EPS) * g


def _gelu(x):
    return 0.5 * x * (1.0 + jnp.tanh(math.sqrt(2.0 / math.pi) * (x + 0.044715 * (x * x * x))))


def _sigmoid(x):
    return 0.5 * (1.0 + jnp.tanh(0.5 * x))


def _silu(x):
    return x * _sigmoid(x)


def _gmlp_kernel(h_ref, ng_ref, win_ref, lng_ref, lnb_ref, ws_ref, bst_ref, wout_ref, *rest, d_inner, final, fused_in):
    rest = list(rest)
    o_ref = rest.pop()
    fg_ref = rest.pop() if final else None
    x = h_ref[...]
    if fused_in:
        po_ref, pz_ref, pw_ref = rest
        x = x + _dot((po_ref[...].astype(F32) * _silu(pz_ref[...].astype(F32))).astype(BF16), pw_ref[...])
    tm = x.shape[0]
    hn = _rms(x, ng_ref[...]).astype(BF16)
    u = _gelu(_dot(hn, win_ref[:, 0:d_inner]))
    v = _gelu(_dot(hn, win_ref[:, d_inner:2 * d_inner]))
    mu = jnp.mean(v, axis=-1, keepdims=True)
    vc = v - mu
    var = jnp.mean(vc * vc, axis=-1, keepdims=True)
    vn = (vc * lax.rsqrt(var + NORM_EPS) * lng_ref[...] + lnb_ref[...]).astype(BF16)
    z = _dot(hn, win_ref[:, 2 * d_inner:3 * d_inner])
    gate = u * _silu(z)

    gw = d_inner // GMLP_GROUPS
    r = lax.broadcasted_iota(jnp.int32, (GMLP_CHUNK, GMLP_CHUNK), 0)
    c = lax.broadcasted_iota(jnp.int32, (GMLP_CHUNK, GMLP_CHUNK), 1)
    causal = c <= r
    cols = []
    for g in range(GMLP_GROUPS):
        w = jnp.where(causal, ws_ref[g], 0.0).astype(BF16)
        bias = bst_ref[:, g:g + 1]
        rows = []
        for ci in range(tm // GMLP_CHUNK):
            vs = vn[ci * GMLP_CHUNK:(ci + 1) * GMLP_CHUNK, g * gw:(g + 1) * gw]
            rows.append(_dot(w, vs) + bias)
        cols.append(jnp.concatenate(rows, axis=0) if len(rows) > 1 else rows[0])
    s = jnp.concatenate(cols, axis=1)
    y = (gate * s).astype(BF16)
    out = x + _dot(y, wout_ref[...])
    if final:
        out = _rms(out, fg_ref[...])
    o_ref[...] = out


def _gmlp_layer(h, norm_g, w_in, ln_g, ln_b, w_s, b_s, w_out, final_g=None, pre=None):
    t, d = h.shape
    d_inner = w_out.shape[0]
    tm = ROW_BLOCK
    final = final_g is not None
    args = [h, norm_g.reshape(1, d), w_in.astype(BF16), ln_g.reshape(1, d_inner), ln_b.reshape(1, d_inner),
            w_s, b_s.T, w_out.astype(BF16)]
    specs = [pl.BlockSpec((tm, d), lambda i: (i, 0)), _const_spec((1, d)), _const_spec(w_in.shape),
             _const_spec((1, d_inner)), _const_spec((1, d_inner)), _const_spec(w_s.shape),
             _const_spec((GMLP_CHUNK, GMLP_GROUPS)), _const_spec(w_out.shape)]
    if pre is not None:
        po, pz, pw = pre
        args += [po, pz, pw.astype(BF16)]
        specs += [pl.BlockSpec((tm, po.shape[1]), lambda i: (i, 0)), pl.BlockSpec((tm, pz.shape[1]), lambda i: (i, 0)),
                  _const_spec(pw.shape)]
    if final:
        args.append(final_g.reshape(1, d))
        specs.append(_const_spec((1, d)))
    return pl.pallas_call(
        functools.partial(_gmlp_kernel, d_inner=d_inner, final=final, fused_in=pre is not None),
        grid=(t // tm,),
        in_specs=specs,
        out_specs=pl.BlockSpec((tm, d), lambda i: (i, 0)),
        out_shape=jax.ShapeDtypeStruct((t, d), F32),
        compiler_params=_cparams("parallel"),
        name="gmlp_final" if final else "gmlp",
    )(*args)


S5_OCT = LANES // S5_GROUP
S5_ROWS = 512


def _s5_lane_perm():
    n = S5_OCT * LANES
    p = np.zeros((n, n), np.float32)
    for i in range(S5_OCT):
        for j in range(S5_OCT):
            for ch in range(S5_GROUP):
                p[i * LANES + j * S5_GROUP + ch, j * LANES + i * S5_GROUP + ch] = 1.0
    return p


def _block_transpose8(v, lperm):
    t = _dot(jnp.concatenate(v, axis=1), lperm)
    return [t[:, j * LANES:(j + 1) * LANES].astype(BF16) for j in range(S5_OCT)]


def _s5_row_perm(chunks=8):
    n = chunks * S5_Q
    p = np.zeros((n, n), np.float32)
    for c in range(chunks):
        for s in range(S5_Q):
            p[s * chunks + c, c * S5_Q + s] = 1.0
    return p


def _s5_in_kernel(h_ref, ng_ref, win_ref, perm_ref, lperm_ref, ur_ref, z_ref, *, d_inner):
    hn = _rms(h_ref[...], ng_ref[...]).astype(BF16)
    z_ref[...] = _dot(hn, win_ref[:, d_inner:2 * d_inner]).astype(BF16)
    u = _dot(hn, win_ref[:, 0:d_inner]).astype(BF16)
    n_oct = d_inner // LANES
    q = S5_Q
    blk_rows = 8 * q
    assert S5_ROWS == 2 * blk_rows
    ups = [_dot(perm_ref[...], u[cb * blk_rows:(cb + 1) * blk_rows, :]).reshape(q, 8, d_inner) for cb in range(2)]
    tiled = jnp.concatenate(ups, axis=1).astype(BF16)
    xs = []
    for si in range(8):
        xs.append(jnp.concatenate([tiled[sb * 8 + si, :, o * LANES:(o + 1) * LANES]
                                   for o in range(n_oct) for sb in range(q // 8)], axis=0))
    ts = _block_transpose8(xs, lperm_ref[...])
    for o in range(n_oct):
        for gi in range(S5_OCT):
            tiles = [ts[gi][(o * (q // 8) + sb) * 16:(o * (q // 8) + sb + 1) * 16, :] for sb in range(q // 8)]
            ur_ref[o * S5_OCT + gi] = jnp.concatenate(tiles, axis=1)


def _s5_in(h, norm_g, w_in):
    t, d = h.shape
    d_inner = w_in.shape[1] // 2
    g = d_inner // S5_GROUP
    tm = S5_ROWS
    return pl.pallas_call(
        functools.partial(_s5_in_kernel, d_inner=d_inner),
        grid=(t // tm,),
        in_specs=[pl.BlockSpec((tm, d), lambda i: (i, 0)), _const_spec((1, d)), _const_spec(w_in.shape),
                  _const_spec((8 * S5_Q, 8 * S5_Q)), _const_spec((S5_OCT * LANES, S5_OCT * LANES))],
        out_specs=[pl.BlockSpec((g, tm // S5_Q, S5_Q * S5_GROUP), lambda i: (0, i, 0)),
                   pl.BlockSpec((tm, d_inner), lambda i: (i, 0))],
        out_shape=[jax.ShapeDtypeStruct((g, t // S5_Q, S5_Q * S5_GROUP), BF16),
                   jax.ShapeDtypeStruct((t, d_inner), BF16)],
        compiler_params=_cparams("parallel"),
        name="s5_in",
    )(h, norm_g.reshape(1, d), w_in.astype(BF16), jnp.asarray(_s5_row_perm(), BF16),
      jnp.asarray(_s5_lane_perm(), BF16))


S5_MAX_LEVELS = 8
S5_PREP_GROUPS = 8
S5_POW_ROWS = 40


def _s5_prep_kernel(are_ref, aim_ref, ls_ref, bx_ref, by_ref, cr_ref, ci_ref,
                    m_ref, p_ref, rt_ref, la_ref, lb_ref, a_sc):
    q, hh = S5_Q, S5_GROUP
    lane = lax.broadcasted_iota(jnp.int32, (1, LANES), 1)
    lo = lane < S5_STATE
    sgn = jnp.where(lo, -1.0, 1.0).astype(F32)

    j = lax.broadcasted_iota(jnp.int32, (S5_POW_ROWS, LANES), 0).astype(F32)

    def group(gi):
        step = jnp.exp(ls_ref[gi])
        ar, ai = are_ref[gi], aim_ref[gi]
        lr, li = ar * step, ai * step
        mag = jnp.exp(j * lr)
        er, ei = mag * jnp.cos(j * li), mag * jnp.sin(j * li)
        nr, ni = er[1:2] - 1.0, ei[1:2]
        den = 1.0 / (ar * ar + ai * ai)
        cfr = (nr * ar + ni * ai) * den
        cfi = (ni * ar - nr * ai) * den

        bx, by = bx_ref[gi], by_ref[gi]
        bba = cfr * bx + cfi * sgn * by
        bbb = sgn * cfr * by - cfi * bx
        bbt = -(sgn * cfr * bx) - cfi * by

        c1 = cr_ref[gi]
        c2 = sgn * ci_ref[gi]
        ee1 = jnp.where(lo, er, ei)
        ee2 = jnp.where(lo, ei, er)
        for jj in range(q + 1):
            a_sc[gi, jj * hh:(jj + 1) * hh, :] = c1 * ee1[jj:jj + 1] + c2 * ee2[jj:jj + 1]

        for s in range(q):
            e = q - 1 - s
            p_ref[gi, s * hh:(s + 1) * hh, :] = (er[e:e + 1] * bba + ei[e:e + 1] * bbb).astype(BF16)

        rt_ref[gi] = (-sgn * a_sc[gi, hh:(q + 1) * hh, :]).astype(BF16)

        kt = _dot_nt(bbt, a_sc[gi, 0:q * hh, :], precision=lax.Precision.HIGHEST)
        col = lax.broadcasted_iota(jnp.int32, kt.shape, 1)
        for s in range(q):
            if s == 0:
                blk = kt
            else:
                blk = jnp.where(col >= s * hh, pltpu.roll(kt, s * hh, axis=1), 0.0)
            m_ref[gi, s * hh:(s + 1) * hh, :] = blk.astype(BF16)

        pr, pi = er[q:q + 1], ei[q:q + 1]
        for k in range(S5_MAX_LEVELS):
            la_ref[gi, k:k + 1, :] = pr
            lb_ref[gi, k:k + 1, :] = sgn * pi
            pr, pi = pr * pr - pi * pi, 2.0 * pr * pi

    for gi in range(are_ref.shape[0]):
        group(gi)


def _s5_prep(a_re, a_im, log_step, b_re, b_im, c_re, c_im):
    g, p = a_re.shape
    hh = b_re.shape[2]
    qh = S5_Q * hh
    dup = lambda x: jnp.concatenate([x, x], axis=-1)
    are2 = dup(a_re).reshape(g, 1, 2 * p)
    aim2 = dup(a_im).reshape(g, 1, 2 * p)
    ls = log_step.reshape(g, 1, 1)
    brt, bit = jnp.swapaxes(b_re, 1, 2), jnp.swapaxes(b_im, 1, 2)
    bx = jnp.concatenate([brt, bit], axis=-1)
    by = jnp.concatenate([bit, brt], axis=-1)
    gb = S5_PREP_GROUPS
    gspec = lambda *s: pl.BlockSpec((gb,) + s, lambda i: (i,) + (0,) * len(s))
    return pl.pallas_call(
        _s5_prep_kernel,
        grid=(g // gb,),
        in_specs=[gspec(1, 2 * p), gspec(1, 2 * p), gspec(1, 1), gspec(hh, 2 * p), gspec(hh, 2 * p),
                  gspec(hh, 2 * p), gspec(hh, 2 * p)],
        out_specs=[gspec(qh, qh), gspec(qh, 2 * p), gspec(qh, 2 * p), gspec(8, 2 * p), gspec(8, 2 * p)],
        out_shape=[jax.ShapeDtypeStruct((g, qh, qh), BF16), jax.ShapeDtypeStruct((g, qh, 2 * p), BF16),
                   jax.ShapeDtypeStruct((g, qh, 2 * p), BF16), jax.ShapeDtypeStruct((g, 8, 2 * p), F32),
                   jax.ShapeDtypeStruct((g, 8, 2 * p), F32)],
        scratch_shapes=[pltpu.VMEM((gb, (S5_Q + 1) * hh, 2 * p), F32)],
        compiler_params=_cparams("parallel"),
        name="s5_prep",
    )(are2, aim2, ls, bx, by, dup(c_re), dup(c_im))


def _s5_core_kernel(u_ref, m_ref, p_ref, rt_ref, la_ref, lb_ref, d_ref, lperm_ref, y_ref, x_sc, *, nc, levels):
    n = u_ref.shape[1]
    q = S5_Q
    tiles = n // 16
    cidx = lax.broadcasted_iota(jnp.int32, (n, LANES), 0) & (nc - 1)
    for gi in range(S5_OCT):
        u = u_ref[gi]
        z = _dot(u, p_ref[gi])
        la, lb = la_ref[gi], lb_ref[gi]
        for k in range(levels):
            sh = 1 << k
            zs = jnp.where(cidx >= sh, pltpu.roll(z, sh, axis=0), 0.0)
            zw = pltpu.roll(zs, S5_STATE, axis=1)
            z = z + la[k:k + 1] * zs + lb[k:k + 1] * zw
        x = jnp.where(cidx >= 1, pltpu.roll(z, 1, axis=0), 0.0)
        y = _dot(u, m_ref[gi]) + _dot_nt(x.astype(BF16), rt_ref[gi]) + u.astype(F32) * d_ref[gi]
        yb = _gelu(y).astype(BF16)
        for tb in range(q // 8):
            x_sc[tb * n:(tb + 1) * n, gi * LANES:(gi + 1) * LANES] = yb[:, tb * LANES:(tb + 1) * LANES]

    for tb in range(q // 8):
        t = _dot(x_sc[tb * n:(tb + 1) * n, :], lperm_ref[...])
        for ti in range(8):
            blk = t[:, ti * LANES:(ti + 1) * LANES].astype(BF16)
            y_ref[:, tb * 8 + ti, :, :] = blk.reshape(tiles, 16, LANES)


def _s5_core(ur, m, p, rt, la, lb, dt, nc):
    g, n, qh = ur.shape
    levels = (nc - 1).bit_length()
    assert nc == 1 << levels and levels <= S5_MAX_LEVELS and n % nc == 0 and qh == S5_Q * S5_GROUP and n % 16 == 0
    ospec = lambda *s: pl.BlockSpec((S5_OCT,) + s, lambda i: (i,) + (0,) * len(s))
    return pl.pallas_call(
        functools.partial(_s5_core_kernel, nc=nc, levels=levels),
        grid=(g // S5_OCT,),
        in_specs=[ospec(n, qh), ospec(qh, qh), ospec(qh, LANES), ospec(qh, LANES), ospec(8, LANES),
                  ospec(8, LANES), ospec(1, qh), _const_spec((S5_OCT * LANES, S5_OCT * LANES))],
        out_specs=pl.BlockSpec((n // 16, S5_Q, 16, LANES), lambda i: (0, 0, 0, i)),
        out_shape=jax.ShapeDtypeStruct((n // 16, S5_Q, 16, g * S5_GROUP), BF16),
        scratch_shapes=[pltpu.VMEM((n * (qh // LANES), S5_OCT * LANES), BF16)],
        compiler_params=_cparams("parallel"),
        name="s5_core",
    )(ur, m, p, rt, la, lb, dt, jnp.asarray(_s5_lane_perm(), BF16))


def _s5_out_kernel(h_ref, y_ref, z_ref, permt_ref, wg_ref, bg_ref, wo_ref, o_ref):
    blk = permt_ref.shape[0]
    y = jnp.concatenate([_dot(permt_ref[...], y_ref[r:r + blk, :]) for r in range(0, y_ref.shape[0], blk)],
                        axis=0).astype(BF16)
    g = _dot(y, wg_ref[...]) + bg_ref[...]
    y2 = y.astype(F32) * _sigmoid(g) * _silu(z_ref[...].astype(F32))
    o_ref[...] = h_ref[...] + _dot(y2.astype(BF16), wo_ref[...])


def _s5_out(h, y, z, w_glu, b_glu, w_out):
    t, d = h.shape
    d_inner = w_out.shape[0]
    blk = 16 * S5_Q
    tm = ROW_BLOCK
    assert tm % blk == 0
    return pl.pallas_call(
        _s5_out_kernel,
        grid=(t // tm,),
        in_specs=[pl.BlockSpec((tm, d), lambda i: (i, 0)), pl.BlockSpec((tm, d_inner), lambda i: (i, 0)),
                  pl.BlockSpec((tm, d_inner), lambda i: (i, 0)), _const_spec((blk, blk)), _const_spec(w_glu.shape),
                  _const_spec((1, d_inner)), _const_spec(w_out.shape)],
        out_specs=pl.BlockSpec((tm, d), lambda i: (i, 0)),
        out_shape=jax.ShapeDtypeStruct((t, d), F32),
        compiler_params=_cparams("parallel"),
        name="s5_out",
    )(h, y, z, jnp.asarray(_s5_row_perm(16).T, BF16), w_glu.astype(BF16), b_glu.reshape(1, d_inner),
      w_out.astype(BF16))


def _s5_layer(h, bsz, norm_g, w_in, a_re, a_im, log_step, b_re, b_im, c_re, c_im, d_skip, w_glu, b_glu, w_out):
    t, _ = h.shape
    seq = t // bsz
    g, _ = a_re.shape
    hh = b_re.shape[2]
    q = S5_Q
    nc = seq // q
    assert hh == S5_GROUP and seq % S5_ROWS == 0
    ur, z = _s5_in(h, norm_g, w_in)
    m, p, rt, la, lb = _s5_prep(a_re, a_im, log_step, b_re, b_im, c_re, c_im)
    dt = jnp.tile(d_skip.reshape(g, 1, hh), (1, 1, q))
    y = _s5_core(ur, m, p, rt, la, lb, dt, nc)
    return _s5_out(h, y.reshape(t, g * hh), z, w_glu, b_glu, w_out)


def _rope_table_kernel(pos_ref, e_ref, cos_ref, sin_ref):
    p = pos_ref[...].astype(F32)
    ang = p[:, 0:1] * e_ref[0:1, :]
    for k in range(1, 4):
        ang = ang + p[:, k:k + 1] * e_ref[k:k + 1, :]
    cos_ref[...] = jnp.cos(ang)
    sin_ref[...] = jnp.sin(ang)


def _rope_tables(positions):
    t = positions.size
    half = MLA_ROPE // 2
    per_row = LANES // half
    inv_freq = (ROPE_THETA ** (-np.arange(0, MLA_ROPE, 2, dtype=np.float32) / np.float32(MLA_ROPE))).astype(np.float32)
    e = np.zeros((per_row, LANES), np.float32)
    for k in range(per_row):
        e[k, k * half:(k + 1) * half] = inv_freq
    rows = t // per_row
    tr = 512
    cos4, sin4 = pl.pallas_call(
        _rope_table_kernel,
        grid=(rows // tr,),
        in_specs=[pl.BlockSpec((tr, per_row), lambda i: (i, 0)), _const_spec((per_row, LANES))],
        out_specs=[pl.BlockSpec((tr, LANES), lambda i: (i, 0))] * 2,
        out_shape=[jax.ShapeDtypeStruct((rows, LANES), F32)] * 2,
        compiler_params=_cparams("parallel"),
        name="rope_table",
    )(positions.reshape(rows, per_row), jnp.asarray(e))
    return cos4.reshape(t, half), sin4.reshape(t, half)


def _rope128(x, c, s1, s2):
    return x * c - pltpu.roll(x, LANES - MLA_ROPE // 2, axis=1) * s1 + pltpu.roll(x, MLA_ROPE // 2, axis=1) * s2


def _mla_proj_kernel(h_ref, ng_ref, wa_ref, wz_ref, qg_ref, wuq_ref, kg_ref, wukv_ref, cos_ref, sin_ref,
                     z_ref, q_ref, k_ref, v_ref, *, q_rank, kv_rank, scale):
    hn = _rms(h_ref[...], ng_ref[...]).astype(BF16)
    z_ref[...] = _dot(hn, wz_ref[...]).astype(BF16)
    a = _dot(hn, wa_ref[...])
    cos, sin = cos_ref[...], sin_ref[...]
    z1, z2 = jnp.zeros_like(cos), jnp.zeros((cos.shape[0], LANES // 2), F32)
    c = jnp.concatenate([cos, cos, z2], axis=1)
    s1 = jnp.concatenate([sin, z1, z2], axis=1)
    s2 = jnp.concatenate([z1, sin, z2], axis=1)
    cq = _rms(a[:, 0:q_rank], qg_ref[...]).astype(BF16)
    ckv = _rms(a[:, q_rank:q_rank + kv_rank], kg_ref[...]).astype(BF16)
    kr = _rope128(a[:, q_rank + kv_rank:q_rank + kv_rank + LANES], c, s1, s2).astype(BF16)
    q = _dot(cq, wuq_ref[...]) * scale
    kv = _dot(ckv, wukv_ref[...])
    for hd in range(MLA_HEADS):
        o = hd * MLA_QK_PAD
        q_ref[:, o:o + MLA_NOPE] = q[:, o:o + MLA_NOPE].astype(BF16)
        q_ref[:, o + MLA_NOPE:o + MLA_QK_PAD] = _rope128(q[:, o + MLA_NOPE:o + MLA_QK_PAD], c, s1, s2).astype(BF16)
        k_ref[:, o:o + MLA_NOPE] = kv[:, o:o + MLA_NOPE].astype(BF16)
        k_ref[:, o + MLA_NOPE:o + MLA_QK_PAD] = kr
        v_ref[:, hd * MLA_V:(hd + 1) * MLA_V] = kv[:, o + MLA_NOPE:o + MLA_NOPE + MLA_V].astype(BF16)


def _mla_proj(h, norm_g, w_in, q_norm_g, w_uq, kv_norm_g, w_ukv, cos, sin):
    t, d = h.shape
    q_rank, kv_rank = w_uq.shape[0], w_ukv.shape[0]
    n_a = q_rank + kv_rank + MLA_ROPE
    d_inner = w_in.shape[1] - n_a
    tm = ROW_BLOCK
    wa = jnp.pad(w_in[:, :n_a], ((0, 0), (0, LANES - MLA_ROPE))).astype(BF16)
    wz = w_in[:, n_a:].astype(BF16)
    qk = MLA_NOPE + MLA_ROPE
    wuq = jnp.pad(w_uq.reshape(q_rank, MLA_HEADS, qk), ((0, 0), (0, 0), (0, MLA_QK_PAD - qk)))
    wuq = wuq.reshape(q_rank, MLA_HEADS * MLA_QK_PAD).astype(BF16)
    half = MLA_ROPE // 2
    assert half * 4 == LANES and cos.shape == (t, half)
    row = lambda n: pl.BlockSpec((tm, n), lambda i: (i, 0))
    return pl.pallas_call(
        functools.partial(_mla_proj_kernel, q_rank=q_rank, kv_rank=kv_rank, scale=float(qk) ** -0.5 * math.log2(math.e)),
        grid=(t // tm,),
        in_specs=[row(d), _const_spec((1, d)), _const_spec(wa.shape), _const_spec(wz.shape),
                  _const_spec((1, q_rank)), _const_spec(wuq.shape), _const_spec((1, kv_rank)),
                  _const_spec(w_ukv.shape), row(half), row(half)],
        out_specs=[row(d_inner), row(MLA_HEADS * MLA_QK_PAD), row(MLA_HEADS * MLA_QK_PAD), row(MLA_HEADS * MLA_V)],
        out_shape=[jax.ShapeDtypeStruct((t, d_inner), BF16),
                   jax.ShapeDtypeStruct((t, MLA_HEADS * MLA_QK_PAD), BF16),
                   jax.ShapeDtypeStruct((t, MLA_HEADS * MLA_QK_PAD), BF16),
                   jax.ShapeDtypeStruct((t, MLA_HEADS * MLA_V), BF16)],
        compiler_params=_cparams("parallel"),
        name="mla_proj",
    )(h, norm_g.reshape(1, d), wa, wz, q_norm_g.reshape(1, q_rank), wuq, kv_norm_g.reshape(1, kv_rank),
      w_ukv.astype(BF16), cos, sin)


ATT_K = 256
ATT_CQ = 256
ATT_CHAINS = 16
ATT_Q = ATT_CHAINS * ATT_CQ
ATT_LOOP_TILES = 4
ATT_QK_TILES = 2
ATT_AHEAD = 2
ATT_VROWS = MLA_V + 16


def _attn_kernel(q_ref, k_ref, v_ref, o_ref, vxt_sc, acc_sc, m_sc, *, n_q_tiles):
    qi = pl.program_id(2)
    n_kt = vxt_sc.shape[0]

    @pl.when(qi == 0)
    def _():
        row = lax.broadcasted_iota(jnp.int32, (ATT_VROWS - MLA_V, ATT_K), 0)
        ones = jnp.where(row == 0, 1.0, 0.0).astype(BF16)
        for j in range(n_kt):
            vt = v_ref[j * ATT_K:(j + 1) * ATT_K, :].astype(F32).T
            vxt_sc[j, 0:MLA_V, :] = vt.astype(BF16)
            vxt_sc[j, MLA_V:ATT_VROWS, :] = ones

    m_sc[...] = jnp.full_like(m_sc, NEG_INF)
    acc_sc[...] = jnp.zeros_like(acc_sc)

    def qk(c, j, n):
        koff = pl.multiple_of(j * ATT_K, ATT_K)
        return _dot_nt(k_ref[pl.ds(koff, n * ATT_K), :], q_ref[c * ATT_CQ:(c + 1) * ATT_CQ, :])

    def softmax(st, c, mask_off):
        qs = slice(c * ATT_CQ, (c + 1) * ATT_CQ)
        if mask_off is not None:
            kk = lax.broadcasted_iota(jnp.int32, st.shape, 0) + mask_off
            qq = lax.broadcasted_iota(jnp.int32, st.shape, 1)
            st = jnp.where(kk <= qq, st, NEG_INF)
        m_prev = m_sc[:, qs]
        m_new = jnp.maximum(m_prev, jnp.max(st, axis=0, keepdims=True))
        alpha = jnp.exp2(m_prev - m_new)
        pt = jnp.exp2(st - m_new).astype(BF16)
        m_sc[:, qs] = m_new
        return alpha, pt

    def accumulate(c, j, alpha, pt):
        qs = slice(c * ATT_CQ, (c + 1) * ATT_CQ)
        acc_sc[:, qs] = alpha * acc_sc[:, qs] + _dot(vxt_sc[j], pt)

    def run(groups):
        sts = [qk(c, j, len(masks)) for c, j, masks in groups[:ATT_AHEAD]]
        pending = None
        for gidx, (c, j, masks) in enumerate(groups):
            if gidx + ATT_AHEAD < len(groups):
                nxt = groups[gidx + ATT_AHEAD]
                sts.append(qk(nxt[0], nxt[1], len(nxt[2])))
            for i, mask_off in enumerate(masks):
                alpha, pt = softmax(sts[gidx][i * ATT_K:(i + 1) * ATT_K, :], c, mask_off)
                if pending is not None:
                    accumulate(*pending)
                pending = (c, j + i, alpha, pt)
            sts[gidx] = None
        accumulate(*pending)

    def grouped(per_chain):
        chunks = {c: [tl[i:i + ATT_QK_TILES] for i in range(0, len(tl), ATT_QK_TILES)] for c, tl in per_chain.items()}
        out = []
        for r in range(max(len(v) for v in chunks.values())):
            for c, v in chunks.items():
                if r < len(v):
                    out.append((c, v[r][0][0], [m for _, m in v[r]]))
        return out

    tiles_per_step = ATT_Q // ATT_K

    def body(i, carry):
        first = ATT_LOOP_TILES * i
        run(grouped({c: [(first + e, None) for e in range(ATT_LOOP_TILES)] for c in range(ATT_CHAINS)}))
        return carry

    base = qi * tiles_per_step
    if n_q_tiles > 1:
        lax.fori_loop(0, base // ATT_LOOP_TILES, body, 0)
    diag = {}
    for c in range(ATT_CHAINS):
        q_lo, q_hi = c * ATT_CQ, (c + 1) * ATT_CQ - 1
        tl = []
        for e in range(tiles_per_step):
            k_lo, k_hi = e * ATT_K, (e + 1) * ATT_K - 1
            if k_lo <= q_hi:
                tl.append((base + e, None if k_hi <= q_lo else k_lo - q_lo))
        diag[c] = tl
    run(grouped(diag))
    acc = acc_sc[...]
    o = acc[0:MLA_V, :] / acc[MLA_V:MLA_V + 1, :]
    o_ref[...] = o.T.astype(o_ref.dtype)


def _attention(q, k, v, bsz):
    t = q.shape[0]
    seq = t // bsz
    assert seq % ATT_Q == 0
    nq = seq // ATT_Q
    qmap = lambda b, h, i: (b * nq + i, h)
    kmap = lambda b, h, i: (b, h)
    return pl.pallas_call(
        functools.partial(_attn_kernel, n_q_tiles=nq),
        grid=(bsz, MLA_HEADS, nq),
        in_specs=[pl.BlockSpec((ATT_Q, MLA_QK_PAD), qmap), pl.BlockSpec((seq, MLA_QK_PAD), kmap),
                  pl.BlockSpec((seq, MLA_V), kmap)],
        out_specs=pl.BlockSpec((ATT_Q, MLA_V), qmap),
        out_shape=jax.ShapeDtypeStruct((t, MLA_HEADS * MLA_V), BF16),
        scratch_shapes=[pltpu.VMEM((seq // ATT_K, ATT_VROWS, ATT_K), BF16), pltpu.VMEM((ATT_VROWS, ATT_Q), F32),
                        pltpu.VMEM((1, ATT_Q), F32)],
        compiler_params=_cparams("parallel", "parallel", "arbitrary"),
        name="mla_attn",
    )(q, k, v)


def _mla_layer(h, bsz, positions, norm_g, w_in, q_norm_g, w_uq, kv_norm_g, w_ukv):
    cos, sin = _rope_tables(positions)
    z, q, k, v = _mla_proj(h, norm_g, w_in, q_norm_g, w_uq, kv_norm_g, w_ukv, cos, sin)
    return _attention(q, k, v, bsz), z


def kernel(x, positions, l0_norm_g, l0_w_in, l0_ln_g, l0_ln_b, l0_w_s, l0_b_s, l0_w_out, l1_norm_g, l1_w_in, l1_a_re, l1_a_im, l1_log_step, l1_b_re, l1_b_im, l1_c_re, l1_c_im, l1_d_skip, l1_w_glu, l1_b_glu, l1_w_out, l2_norm_g, l2_w_in, l2_q_norm_g, l2_w_uq, l2_kv_norm_g, l2_w_ukv, l2_w_out, l3_norm_g, l3_w_in, l3_ln_g, l3_ln_b, l3_w_s, l3_b_s, l3_w_out, final_norm_g):
    bsz, seq, d = x.shape
    h = x.reshape(bsz * seq, d)
    h = _gmlp_layer(h, l0_norm_g, l0_w_in, l0_ln_g, l0_ln_b, l0_w_s, l0_b_s, l0_w_out)
    h = _s5_layer(h, bsz, l1_norm_g, l1_w_in, l1_a_re, l1_a_im, l1_log_step, l1_b_re, l1_b_im, l1_c_re, l1_c_im,
                  l1_d_skip, l1_w_glu, l1_b_glu, l1_w_out)
    o, z = _mla_layer(h, bsz, positions, l2_norm_g, l2_w_in, l2_q_norm_g, l2_w_uq, l2_kv_norm_g, l2_w_ukv)
    h = _gmlp_layer(h, l3_norm_g, l3_w_in, l3_ln_g, l3_ln_b, l3_w_s, l3_b_s, l3_w_out, final_g=final_norm_g,
                    pre=(o, z, l2_w_out))
    return h.reshape(bsz, seq, d)
```
